```python
import functools
import jax, jax.numpy as jnp
from jax import lax
import numpy as np

D_MODEL = 1024
BATCH = 16
SEQ = 2048
DEPTH = 1
DEC_BATCH = 128
DEC_SEQ = 1
PAST_LEN = 8192
PAGE_SIZE = 128

HEAD_DIM = 64
D_MIX = D_MODEL
A_HEADS = (D_MIX // 2) // HEAD_DIM
A_KV_HEADS = 1
B_HEADS = (D_MIX - A_HEADS * HEAD_DIM) // HEAD_DIM
B_KV_HEADS = 2
IDX_HEADS = 8
IDX_DIM = 64
TOPK_MAX = 256
ROPE_DIM = HEAD_DIM // 4
IDX_ROPE_DIM = IDX_DIM // 4
ROPE_THETA = 500000.0
N_META = 16
D_FF = ((8 * D_MODEL + 3 * 256 - 1) // (3 * 256)) * 256
Q_BLOCK = 128
LN_EPS = 1e-5
ALPHA = (2.0 * DEPTH) ** 0.25
BETA = (8.0 * DEPTH) ** -0.25
FORGET_BIAS_INIT = 3.0
FORGET_W_SCALE = 0.5
IN_SIZES = (A_HEADS * HEAD_DIM, A_KV_HEADS * HEAD_DIM, A_KV_HEADS * HEAD_DIM,
            IDX_HEADS * IDX_DIM, IDX_DIM, IDX_HEADS,
            B_HEADS * HEAD_DIM, B_KV_HEADS * HEAD_DIM, B_KV_HEADS * HEAD_DIM, B_HEADS)
N_IN = sum(IN_SIZES)

kernel_name = 'hybrid_dsa_fox_decode_step'


def _split_points():
    pts, acc = [], 0
    for s in IN_SIZES[:-1]:
        acc += s
        pts.append(acc)
    return pts


def layer_norm(x, g, b):
    xf = x.astype(jnp.float32)
    mu = xf.mean(-1, keepdims=True)
    var = jnp.square(xf - mu).mean(-1, keepdims=True)
    return ((xf - mu) * lax.rsqrt(var + LN_EPS) * g.astype(jnp.float32) + b.astype(jnp.float32)).astype(x.dtype)


def partial_rope(x, pos, rot_dim):
    half = rot_dim // 2
    inv = ROPE_THETA ** (-jnp.arange(half, dtype=jnp.float32) / half)
    ang = pos.astype(jnp.float32)[:, None] * inv
    cos, sin = jnp.cos(ang)[:, None, :], jnp.sin(ang)[:, None, :]
    xf = x[..., :rot_dim].astype(jnp.float32)
    x1, x2 = xf[..., :half], xf[..., half:]
    rot = jnp.concatenate([x1 * cos - x2 * sin, x2 * cos + x1 * sin], -1).astype(x.dtype)
    return jnp.concatenate([rot, x[..., rot_dim:]], -1)


def mixer_inputs(h, w_in, b_forget, pos):
    bsz, s, _ = h.shape
    qa, ka, va, qi, ki, wi, qb, kb, vb, fl = jnp.split(h @ w_in, _split_points(), axis=-1)
    qa = partial_rope(qa.reshape(bsz, s, A_HEADS, HEAD_DIM), pos, ROPE_DIM)
    ka = partial_rope(ka.reshape(bsz, s, A_KV_HEADS, HEAD_DIM), pos, ROPE_DIM)
    va = va.reshape(bsz, s, A_KV_HEADS, HEAD_DIM)
    qi = partial_rope(qi.reshape(bsz, s, IDX_HEADS, IDX_DIM), pos, IDX_ROPE_DIM)
    ki = partial_rope(ki[:, :, None, :], pos, IDX_ROPE_DIM)[:, :, 0, :]
    qb = qb.reshape(bsz, s, B_HEADS, HEAD_DIM)
    kb = kb.reshape(bsz, s, B_KV_HEADS, HEAD_DIM)
    vb = vb.reshape(bsz, s, B_KV_HEADS, HEAD_DIM)
    lf = jax.nn.log_sigmoid((fl + b_forget).astype(jnp.float32))
    a_kv = jnp.stack([ka, va], axis=2)
    b_kv = jnp.stack([kb, vb], axis=2)
    return qa, a_kv, qi, ki, wi, qb, b_kv, lf


def indexer_scores(qi, wi, ki):
    dots = jnp.einsum('bqhd,bkd->bqhk', qi, ki, preferred_element_type=jnp.float32) * IDX_DIM ** -0.5
    return jnp.einsum('bqhk,bqh->bqk', jax.nn.relu(dots), wi.astype(jnp.float32) * IDX_HEADS ** -0.5)


def select_topk(scores, admissible, k):
    vals, idx = lax.top_k(jnp.where(admissible, scores, -jnp.inf), k)
    return idx, jnp.isfinite(vals)


def sparse_attend(qa, kv_sel, valid):
    b, q = qa.shape[:2]
    qg = qa.reshape(b, q, A_KV_HEADS, A_HEADS // A_KV_HEADS, HEAD_DIM)
    logits = jnp.einsum('bqgrd,bqkgd->bqgrk', qg, kv_sel[:, :, :, 0],
                        preferred_element_type=jnp.float32) * HEAD_DIM ** -0.5
    logits = jnp.where(valid[:, :, None, None, :], logits, -jnp.inf)
    p = jax.nn.softmax(logits, axis=-1).astype(qa.dtype)
    o = jnp.einsum('bqgrk,bqkgd->bqgrd', p, kv_sel[:, :, :, 1])
    return o.reshape(b, q, A_HEADS * HEAD_DIM)


def dsa_prompt(qa, a_kv, qi, ki, wi, topk):
    b, lp = qa.shape[:2]
    kpos = jnp.arange(lp)

    def block(i):
        qs = i * Q_BLOCK
        sl = lambda t: lax.dynamic_slice_in_dim(t, qs, Q_BLOCK, axis=1)
        qpos = qs + jnp.arange(Q_BLOCK)
        sc = indexer_scores(sl(qi), sl(wi), ki)
        idx, valid = select_topk(sc, (kpos[None, :] <= qpos[:, None])[None], topk)
        kv_sel = jax.vmap(lambda kv, ix: kv[ix])(a_kv, idx)
        return sparse_attend(sl(qa), kv_sel, valid)

    out = lax.map(block, jnp.arange(lp // Q_BLOCK))
    return out.transpose(1, 0, 2, 3).reshape(b, lp, -1)


def fox_prompt(qb, b_kv, lf):
    b, lp = qb.shape[:2]
    r = B_HEADS // B_KV_HEADS
    c = jnp.cumsum(lf, axis=1)
    c_k = c.transpose(0, 2, 1).reshape(b, B_KV_HEADS, r, 1, lp)
    kb, vb = b_kv[:, :, 0], b_kv[:, :, 1]
    kpos = jnp.arange(lp)

    def block(i):
        qs = i * Q_BLOCK
        q = lax.dynamic_slice_in_dim(qb, qs, Q_BLOCK, 1).reshape(b, Q_BLOCK, B_KV_HEADS, r, HEAD_DIM)
        c_q = lax.dynamic_slice_in_dim(c, qs, Q_BLOCK, 1).transpose(0, 2, 1).reshape(b, B_KV_HEADS, r, Q_BLOCK, 1)
        logits = jnp.einsum('bqgrd,bkgd->bgrqk', q, kb, preferred_element_type=jnp.float32) * HEAD_DIM ** -0.5
        logits = logits + (c_q - c_k)
        qpos = qs + jnp.arange(Q_BLOCK)
        logits = jnp.where(kpos[None, :] <= qpos[:, None], logits, -jnp.inf)
        p = jax.nn.softmax(logits, axis=-1).astype(qb.dtype)
        return jnp.einsum('bgrqk,bkgd->bqgrd', p, vb).reshape(b, Q_BLOCK, B_HEADS * HEAD_DIM)

    out = lax.map(block, jnp.arange(lp // Q_BLOCK))
    return out.transpose(1, 0, 2, 3).reshape(b, lp, -1)


def dsa_sample(qa, a_kv, qi, ki, wi, pos, cache_a_kv, cache_idx_k, page_table, l, topk):
    db, ds = qa.shape[:2]
    past = page_table.shape[1] * PAGE_SIZE
    ki_past = cache_idx_k[l, page_table].reshape(db, past, IDX_DIM)
    sc = jnp.concatenate([indexer_scores(qi, wi, ki_past), indexer_scores(qi, wi, ki)], -1)
    kpos = jnp.arange(past + ds)
    idx, valid = select_topk(sc, (kpos[None, :] <= pos[:, None])[None], topk)
    from_past = idx < past
    pi = jnp.minimum(idx, past - 1)
    phys = jnp.take_along_axis(page_table, (pi // PAGE_SIZE).reshape(db, -1), axis=1).reshape(pi.shape)
    kv_past = cache_a_kv[l, phys, pi % PAGE_SIZE].astype(a_kv.dtype)
    kv_new = jax.vmap(lambda kv, ix: kv[ix])(a_kv, jnp.clip(idx - past, 0, ds - 1))
    kv_sel = jnp.where(from_past[..., None, None, None], kv_past, kv_new)
    return sparse_attend(qa, kv_sel, valid)


def fox_sample(qb, b_kv, lf, cache_b_kv, cache_b_logf, page_table, l):
    db, ds = qb.shape[:2]
    past = page_table.shape[1] * PAGE_SIZE
    r = B_HEADS // B_KV_HEADS
    kv_past = cache_b_kv[l, page_table].reshape(db, past, 2, B_KV_HEADS, HEAD_DIM).astype(b_kv.dtype)
    lf_past = cache_b_logf[l, page_table].reshape(db, past, B_HEADS).astype(jnp.float32)
    n = jnp.cumsum(lf, axis=1)
    rev = lax.cumsum(lf_past, axis=1, reverse=True)
    rp = jnp.concatenate([rev[:, 1:], jnp.zeros_like(rev[:, :1])], 1)
    n_q = n.transpose(0, 2, 1).reshape(db, B_KV_HEADS, r, ds, 1)
    bias_past = n_q + rp.transpose(0, 2, 1).reshape(db, B_KV_HEADS, r, 1, past)
    bias_new = n_q - n.transpose(0, 2, 1).reshape(db, B_KV_HEADS, r, 1, ds)
    q = qb.reshape(db, ds, B_KV_HEADS, r, HEAD_DIM)
    scale = HEAD_DIM ** -0.5
    lg_past = jnp.einsum('bqgrd,bkgd->bgrqk', q, kv_past[:, :, 0], preferred_element_type=jnp.float32) * scale + bias_past
    lg_new = jnp.einsum('bqgrd,bkgd->bgrqk', q, b_kv[:, :, 0], preferred_element_type=jnp.float32) * scale + bias_new
    tq = jnp.arange(ds)
    lg_new = jnp.where(tq[None, :] <= tq[:, None], lg_new, -jnp.inf)
    p = jax.nn.softmax(jnp.concatenate([lg_past, lg_new], -1), axis=-1).astype(qb.dtype)
    o = (jnp.einsum('bgrqk,bkgd->bqgrd', p[..., :past], kv_past[:, :, 1])
         + jnp.einsum('bgrqk,bkgd->bqgrd', p[..., past:], b_kv[:, :, 1]))
    return o.reshape(db, ds, B_HEADS * HEAD_DIM)


def residual_and_ffn(h, mixed, ln1_g, ln1_b, w_gate, w_up, w_down, ln2_g, ln2_b):
    h = layer_norm(ALPHA * h + mixed, ln1_g, ln1_b)
    ffn = (jax.nn.silu(h @ w_gate) * (h @ w_up)) @ w_down
    return layer_norm(ALPHA * h + ffn, ln2_g, ln2_b)


def pad_seq(t, length):
    return jnp.pad(t, [(0, 0), (0, length - t.shape[1])] + [(0, 0)] * (t.ndim - 2))


def setup_inputs(seed: int = 0) -> dict:
    key = jax.random.key(seed)
    ks = jax.random.split(key, 32)
    n_pages = PAST_LEN // PAGE_SIZE
    n_used = DEC_BATCH * n_pages
    n_pool = n_used + n_used // 4

    def nrm(k, shape, scale=1.0):
        return jax.random.normal(k, shape, jnp.float32) * scale

    x_prompt = nrm(ks[0], (BATCH, SEQ, D_MODEL))
    x_sample = nrm(ks[1], (DEC_BATCH, DEC_SEQ, D_MODEL))
    cache_a_kv = nrm(ks[2], (DEPTH, n_pool, PAGE_SIZE, 2, A_KV_HEADS, HEAD_DIM))
    cache_idx_k = nrm(ks[3], (DEPTH, n_pool, PAGE_SIZE, IDX_DIM))
    cache_b_kv = nrm(ks[4], (DEPTH, n_pool, PAGE_SIZE, 2, B_KV_HEADS, HEAD_DIM))
    cache_b_logf = jax.nn.log_sigmoid(FORGET_BIAS_INIT + nrm(ks[5], (DEPTH, n_pool, PAGE_SIZE, B_HEADS)))
    page_table = jax.random.permutation(ks[6], n_pool)[:n_used].reshape(DEC_BATCH, n_pages).astype(jnp.int32)
    meta = nrm(ks[7], (N_META, D_MODEL))
    ln_emb_g = 1.0 + nrm(ks[8], (D_MODEL,), 0.02)
    ln_emb_b = nrm(ks[9], (D_MODEL,), 0.02)
    col_scales = (1.0, 1.0, BETA, 1.0, 1.0, 1.0, 1.0, 1.0, BETA, FORGET_W_SCALE)
    w_in = jnp.concatenate([nrm(ks[10 + i], (DEPTH, D_MODEL, sz), D_MODEL ** -0.5 * sc)
                            for i, (sz, sc) in enumerate(zip(IN_SIZES, col_scales))], -1)
    b_forget = FORGET_BIAS_INIT + nrm(ks[20], (DEPTH, B_HEADS), 0.1)
    w_o = nrm(ks[21], (DEPTH, D_MIX, D_MODEL), D_MIX ** -0.5 * BETA)
    ln1_g = 1.0 + nrm(ks[22], (DEPTH, D_MODEL), 0.02)
    ln1_b = nrm(ks[23], (DEPTH, D_MODEL), 0.02)
    w_gate = nrm(ks[24], (DEPTH, D_MODEL, D_FF), D_MODEL ** -0.5)
    w_up = nrm(ks[25], (DEPTH, D_MODEL, D_FF), D_MODEL ** -0.5 * BETA)
    w_down = nrm(ks[26], (DEPTH, D_FF, D_MODEL), D_FF ** -0.5 * BETA)
    ln2_g = 1.0 + nrm(ks[27], (DEPTH, D_MODEL), 0.02)
    ln2_b = nrm(ks[28], (DEPTH, D_MODEL), 0.02)
    return {'x_prompt': x_prompt, 'x_sample': x_sample,
            'cache_a_kv': cache_a_kv, 'cache_idx_k': cache_idx_k,
            'cache_b_kv': cache_b_kv, 'cache_b_logf': cache_b_logf, 'page_table': page_table,
            'meta': meta, 'ln_emb_g': ln_emb_g, 'ln_emb_b': ln_emb_b,
            'w_in': w_in, 'b_forget': b_forget, 'w_o': w_o, 'ln1_g': ln1_g, 'ln1_b': ln1_b,
            'w_gate': w_gate, 'w_up': w_up, 'w_down': w_down, 'ln2_g': ln2_g, 'ln2_b': ln2_b}


def reference(x_prompt, x_sample, cache_a_kv, cache_idx_k, cache_b_kv, cache_b_logf, page_table,
              meta, ln_emb_g, ln_emb_b, w_in, b_forget, w_o, ln1_g, ln1_b,
              w_gate, w_up, w_down, ln2_g, ln2_b):
    bsz, seq, _ = x_prompt.shape
    db, ds, _ = x_sample.shape
    past = page_table.shape[1] * PAGE_SIZE
    total = N_META + seq
    lp = -(-total // Q_BLOCK) * Q_BLOCK
    topk_p = min(TOPK_MAX, seq // 4)
    topk_s = min(TOPK_MAX, (past + ds) // 4)
    pos_p = jnp.arange(total)
    pos_s = past + jnp.arange(ds)
    pad = functools.partial(pad_seq, length=lp)

    meta_b = jnp.broadcast_to(meta[None].astype(x_prompt.dtype), (bsz, N_META, D_MODEL))
    hp = layer_norm(jnp.concatenate([meta_b, x_prompt], 1), ln_emb_g, ln_emb_b)
    hs = layer_norm(x_sample, ln_emb_g, ln_emb_b)

    a_kv_p, idx_p, b_kv_p, lf_p = [], [], [], []
    a_kv_s, idx_s, b_kv_s, lf_s = [], [], [], []
    for l in range(DEPTH):
        qa, a_kv, qi, ki, wi, qb, b_kv, lf = mixer_inputs(hp, w_in[l], b_forget[l], pos_p)
        a_kv_p.append(a_kv); idx_p.append(ki); b_kv_p.append(b_kv); lf_p.append(lf)
        oa = dsa_prompt(pad(qa), pad(a_kv), pad(qi), pad(ki), pad(wi), topk_p)[:, :total]
        ob = fox_prompt(pad(qb), pad(b_kv), pad(lf))[:, :total]
        mixed = jnp.concatenate([oa, ob], -1) @ w_o[l]
        hp = residual_and_ffn(hp, mixed, ln1_g[l], ln1_b[l], w_gate[l], w_up[l], w_down[l], ln2_g[l], ln2_b[l])

        qa, a_kv, qi, ki, wi, qb, b_kv, lf = mixer_inputs(hs, w_in[l], b_forget[l], pos_s)
        a_kv_s.append(a_kv); idx_s.append(ki); b_kv_s.append(b_kv); lf_s.append(lf)
        oa = dsa_sample(qa, a_kv, qi, ki, wi, pos_s, cache_a_kv, cache_idx_k, page_table, l, topk_s)
        ob = fox_sample(qb, b_kv, lf, cache_b_kv, cache_b_logf, page_table, l)
        mixed = jnp.concatenate([oa, ob], -1) @ w_o[l]
        hs = residual_and_ffn(hs, mixed, ln1_g[l], ln1_b[l], w_gate[l], w_up[l], w_down[l], ln2_g[l], ln2_b[l])

    return (hp[:, N_META:], hs,
            jnp.stack(a_kv_p), jnp.stack(idx_p), jnp.stack(b_kv_p), jnp.stack(lf_p),
            jnp.stack(a_kv_s), jnp.stack(idx_s), jnp.stack(b_kv_s), jnp.stack(lf_s))
```

```python
import functools

import jax
import jax.numpy as jnp
from jax import lax
from jax.experimental import pallas as pl
from jax.experimental.pallas import tpu as pltpu

HEAD_DIM = 64
N_HEADS = 8
B_KV_HEADS = 2
ROPE_HALF = 8
ROPE_THETA = 500000.0
TOPK_MAX = 256
PAGE = 128
LN_EPS = 1e-5
Q_BLOCK = 128
KEY_TILE = 256
LANES = 128
INT_MIN = -2 ** 31
NEG_INF = float("-inf")
VMEM_LIMIT = 56 * 1024 * 1024

F32 = jnp.float32
BF16 = jnp.bfloat16
MXU_DTYPE = BF16

_NT = (((1,), (1,)), ((), ()))


def _dot(a, b):
    return jnp.dot(a, b, preferred_element_type=F32)


def _dot_nt(a, b):
    return lax.dot_general(a, b, _NT, preferred_element_type=F32)


def _layer_norm(x, g, b):
    mu = jnp.mean(x, axis=-1, keepdims=True)
    xc = x - mu
    var = jnp.mean(xc * xc, axis=-1, keepdims=True)
    return xc * lax.rsqrt(var + LN_EPS) * g + b


def _split3(x):
    hi = x.astype(MXU_DTYPE)
    r1 = x - hi.astype(F32)
    mid = r1.astype(MXU_DTYPE)
    lo = (r1 - mid.astype(F32)).astype(MXU_DTYPE)
    return hi, mid, lo


_C_QA, _C_QI, _C_QB, _C_KVA, _C_KIW, _C_KVB, _C_END = 0, 512, 1024, 1536, 1664, 1792, 2048


def _rope(v, cos, sa, sb):
    return v * cos + pltpu.roll(v, LANES - ROPE_HALF, 1) * sa + pltpu.roll(v, ROPE_HALF, 1) * sb


def _proj_body(x_ref, g_ref, b_ref, w_ref, bf_ref, cos1_ref, sa1_ref, sb1_ref, cos2_ref, sa2_ref, sb2_ref,
               tri_ref, c0_ref,
               q_ref, wc_ref, kidx_ref, kva_ref, kvb_ref, akv_ref, idxk_ref, bkv_ref, lf_ref,
               carry_ref):
    j = pl.program_id(1)
    tm = x_ref.shape[1]
    h = _layer_norm(x_ref[0], g_ref[...], b_ref[...])
    proj = _dot(h.astype(MXU_DTYPE), w_ref[...])

    cos1, sa1, sb1 = cos1_ref[...], sa1_ref[...], sb1_ref[...]
    scale = HEAD_DIM ** -0.5
    for g in range(_C_QB // LANES):
        seg = proj[:, g * LANES:(g + 1) * LANES]
        q_ref[0, :, g * LANES:(g + 1) * LANES] = (_rope(seg, cos1, sa1, sb1) * scale).astype(q_ref.dtype)
    q_ref[0, :, _C_QB:_C_KVA] = (proj[:, _C_QB:_C_KVA] * scale).astype(q_ref.dtype)

    cos2, sa2, sb2 = cos2_ref[...], sa2_ref[...], sb2_ref[...]
    kva = _rope(proj[:, _C_KVA:_C_KIW], cos2, sa2, sb2)
    akv_ref[0] = kva
    kva_ref[0] = kva.astype(kva_ref.dtype)
    kiw = _rope(proj[:, _C_KIW:_C_KVB], cos2, sa2, sb2)
    idxk_ref[0] = kiw[:, :HEAD_DIM]
    kidx_ref[0] = kiw[:, :HEAD_DIM].astype(kidx_ref.dtype)
    wi = kiw[:, HEAD_DIM:HEAD_DIM + N_HEADS] * (N_HEADS ** -0.5)
    z = kiw[:, HEAD_DIM + N_HEADS:HEAD_DIM + 2 * N_HEADS] + bf_ref[...]
    lf = jnp.minimum(z, 0.0) - jnp.log1p(jnp.exp(-jnp.abs(z)))
    lf_ref[0] = lf
    kvb = proj[:, _C_KVB:_C_END]
    bkv_ref[0] = kvb
    kvb_ref[0] = kvb.astype(kvb_ref.dtype)

    @pl.when(j == 0)
    def _():
        carry_ref[...] = c0_ref[...]

    tri = tri_ref[...]
    hi, mid, lo = _split3(lf)
    cs = _dot(tri, hi) + _dot(tri, mid) + _dot(tri, lo) + carry_ref[...]
    carry_ref[...] = cs[tm - 1:tm, :]
    wc_ref[0] = jnp.concatenate([wi, cs], axis=1)


def _rope_tables(pos):
    inv = ROPE_THETA ** (-jnp.arange(ROPE_HALF, dtype=F32) / ROPE_HALF)
    ang = pos.astype(F32)[:, None] * inv
    cos, sin = jnp.cos(ang), jnp.sin(ang)
    ones = jnp.ones((pos.shape[0], HEAD_DIM - 2 * ROPE_HALF), F32)
    zeros = jnp.zeros_like(ones)
    zh = jnp.zeros_like(sin)
    cos_h = jnp.concatenate([cos, cos, ones], 1)
    sa_h = jnp.concatenate([-sin, zh, zeros], 1)
    sb_h = jnp.concatenate([zh, sin, zeros], 1)
    one_h, zero_h = jnp.ones_like(cos_h), jnp.zeros_like(cos_h)
    t1 = [jnp.concatenate([t, t], 1) for t in (cos_h, sa_h, sb_h)]
    t2 = [jnp.concatenate([cos_h, one_h], 1), jnp.concatenate([sa_h, zero_h], 1),
          jnp.concatenate([sb_h, zero_h], 1)]
    return t1 + t2


def _proj_call(x3, pos, c0, w_cat, ln_g, ln_b, b_forget, tm):
    nb, s, d = x3.shape
    assert s % tm == 0
    tables = _rope_tables(pos)
    tri = jnp.tril(jnp.ones((tm, tm), F32)).astype(MXU_DTYPE)
    row = lambda b, j: (b, j, 0)
    const2 = lambda b, j: (0, 0)
    tab_spec = pl.BlockSpec((tm, LANES), lambda b, j: (j, 0))
    widths = (3 * 512, 2 * N_HEADS, HEAD_DIM, 2 * HEAD_DIM, 4 * HEAD_DIM, 2 * HEAD_DIM, HEAD_DIM, 4 * HEAD_DIM,
              N_HEADS)
    dtypes = (MXU_DTYPE, F32, MXU_DTYPE, MXU_DTYPE, MXU_DTYPE, F32, F32, F32, F32)
    return pl.pallas_call(
        _proj_body,
        grid=(nb, s // tm),
        in_specs=[pl.BlockSpec((1, tm, d), row),
                  pl.BlockSpec((1, d), const2), pl.BlockSpec((1, d), const2),
                  pl.BlockSpec(w_cat.shape, const2),
                  pl.BlockSpec((1, N_HEADS), const2)] + [tab_spec] * 6 +
                 [pl.BlockSpec((tm, tm), const2), pl.BlockSpec((1, N_HEADS), const2)],
        out_specs=[pl.BlockSpec((1, tm, w), row) for w in widths],
        out_shape=[jax.ShapeDtypeStruct((nb, s, w), dt) for w, dt in zip(widths, dtypes)],
        scratch_shapes=[pltpu.VMEM((1, N_HEADS), F32)],
        compiler_params=pltpu.CompilerParams(dimension_semantics=("arbitrary", "arbitrary"),
                                             vmem_limit_bytes=VMEM_LIMIT),
        name="proj",
    )(x3, ln_g.reshape(1, d), ln_b.reshape(1, d), w_cat, b_forget.reshape(1, N_HEADS), *tables, tri, c0)


def _sortable_key(s):
    bits = pltpu.bitcast(s, jnp.int32)
    return bits ^ ((bits >> 31) & jnp.int32(0x7FFFFFFF))


def _select_topk(keys_ref, selb_ref, ut_ref, n_tiles, k):
    width = keys_ref.shape[2]
    halves = width // LANES
    kf = float(k)

    def count(pred):
        def body(t, acc):
            kt = keys_ref[t]
            for hh in range(halves):
                acc = acc + jnp.where(pred(kt[:, hh * LANES:(hh + 1) * LANES]), 1.0, 0.0)
            return acc
        acc = lax.fori_loop(0, n_tiles, body, jnp.zeros((Q_BLOCK, LANES), F32))
        return jnp.sum(acc, axis=1, keepdims=True)

    def bcast(col):
        return jnp.broadcast_to(col, (Q_BLOCK, LANES))

    def n_unsettled(cnt):
        return jnp.sum(jnp.where(cnt != kf, 1.0, 0.0))

    floor_b = jnp.full((Q_BLOCK, LANES), INT_MIN + 1, jnp.int32)
    n_adm = count(lambda x: x >= floor_b)
    cur0 = jnp.full((Q_BLOCK, 1), INT_MIN, jnp.int32)
    cnt0 = jnp.where(n_adm <= kf, kf, n_adm)

    def cond(st):
        it, _, _, bad = st
        return jnp.logical_and(it < 32, bad > 0.0)

    def body(st):
        it, cur, cnt, _ = st
        cand = cur + jnp.left_shift(jnp.int32(1), 31 - it)
        cand_b = bcast(cand)
        c = count(lambda x: x >= cand_b)
        take = c >= kf
        cur = jnp.where(take, cand, cur)
        cnt = jnp.where(take, c, cnt)
        return it + 1, cur, cnt, n_unsettled(cnt)

    _, cur, cnt, _ = lax.while_loop(cond, body, (jnp.int32(0), cur0, cnt0, n_unsettled(cnt0)))
    has_ties = jnp.sum(jnp.where(cnt > kf, 1.0, 0.0)) > 0.0

    @pl.when(jnp.logical_not(has_ties))
    def _():
        thr_b = bcast(jnp.maximum(cur, INT_MIN + 1))

        def body(t, carry):
            kt = keys_ref[t]
            for hh in range(halves):
                sl = slice(hh * LANES, (hh + 1) * LANES)
                selb_ref[t, :, sl] = jnp.where(kt[:, sl] >= thr_b, 0.0, NEG_INF)
            return carry
        lax.fori_loop(0, n_tiles, body, 0)

    @pl.when(has_ties)
    def _():
        cur_b = bcast(cur)
        need = kf - count(lambda x: x > cur_b)
        cur_w = jnp.broadcast_to(cur, (Q_BLOCK, width))
        ut = ut_ref[...]

        def body(t, seen):
            kt = keys_ref[t]
            eqf = jnp.where(kt == cur_w, jnp.where(kt > INT_MIN, 1.0, 0.0), 0.0)
            rank = _dot(eqf.astype(MXU_DTYPE), ut) + seen
            keep_tie = jnp.where(rank <= need, eqf, 0.0)
            selb_ref[t] = jnp.where(kt > cur_w, 0.0, jnp.where(keep_tie > 0.0, 0.0, NEG_INF))
            return seen + jnp.sum(eqf, axis=1, keepdims=True)
        lax.fori_loop(0, n_tiles, body, jnp.zeros((Q_BLOCK, 1), F32))


def _prompt_attn_body(qcat_ref, wc_ref, kidx_ref, kva_ref, kvb_ref, ckt_ref,
                      kidxm_ref, kvam_ref, kvbm_ref, cktm_ref, ut_ref,
                      o_ref,
                      kidx_c, kva_c, kvb_c, ckt_c, keys_ref, cmask_ref, selb_ref, lg_ref, m_ref, l_ref, acc_ref,
                      wb_ref, cqb_ref, *, n_meta, topk):
    j = pl.program_id(1)
    s = kidx_ref.shape[1]
    kp = kidx_c.shape[0]
    n_u = keys_ref.shape[0]
    rows = N_HEADS * Q_BLOCK

    @pl.when(j == 0)
    def _():
        for dst, m_src, x_src in ((kidx_c, kidxm_ref, kidx_ref), (kva_c, kvam_ref, kva_ref),
                                  (kvb_c, kvbm_ref, kvb_ref)):
            dst[0:LANES, :] = m_src[...]
            dst[LANES:LANES + s, :] = x_src[0]
            if kp > LANES + s:
                dst[LANES + s:, :] = jnp.zeros((kp - LANES - s, dst.shape[1]), dst.dtype)
        ck = jnp.concatenate([cktm_ref[...], ckt_ref[0], jnp.zeros((N_HEADS, kp - LANES - s), F32)], axis=1)
        for u in range(n_u):
            ckt_c[u] = ck[:, u * KEY_TILE:(u + 1) * KEY_TILE]

    q = qcat_ref[0]

    def stack_heads(off):
        return jnp.concatenate([q[:, off + h * HEAD_DIM:off + (h + 1) * HEAD_DIM] for h in range(N_HEADS)], axis=0)

    qa_all, qi_all, qb_all = stack_heads(_C_QA), stack_heads(_C_QI), stack_heads(_C_QB)
    z_all = jnp.zeros((rows, HEAD_DIM), q.dtype)
    z_half = jnp.zeros((rows // 2, HEAD_DIM), q.dtype)
    qa_pad = jnp.concatenate([qa_all, z_all], axis=1)
    qb_bd = jnp.concatenate([jnp.concatenate([qb_all[:rows // 2], z_half], axis=1),
                             jnp.concatenate([z_half, qb_all[rows // 2:]], axis=1)], axis=0)
    wcv = wc_ref[0]
    for h in range(N_HEADS):
        wb_ref[h] = jnp.broadcast_to(wcv[:, h:h + 1], (Q_BLOCK, KEY_TILE))
        cqb_ref[h] = jnp.broadcast_to(wcv[:, N_HEADS + h:N_HEADS + h + 1], (Q_BLOCK, KEY_TILE))

    n_tiles = jnp.right_shift(j + 3, 1)

    def tile_rows(u):
        return pl.ds(pl.multiple_of(u * KEY_TILE, KEY_TILE), KEY_TILE)

    def idx_body(u, carry):
        d = _dot_nt(qi_all, kidx_c[tile_rows(u), :])
        sc = jnp.zeros((Q_BLOCK, KEY_TILE), F32)
        for h in range(N_HEADS):
            sc = sc + wb_ref[h] * jnp.maximum(d[h * Q_BLOCK:(h + 1) * Q_BLOCK], 0.0)
        col = u * KEY_TILE + lax.broadcasted_iota(jnp.int32, (Q_BLOCK, KEY_TILE), 1)
        row = lax.broadcasted_iota(jnp.int32, (Q_BLOCK, KEY_TILE), 0)
        valid = jnp.logical_or(col < n_meta,
                               jnp.logical_and(col >= LANES, col - LANES <= j * Q_BLOCK + row))
        keys_ref[u] = jnp.where(valid, _sortable_key(sc), INT_MIN)
        cmask_ref[u] = jnp.where(valid, 0.0, NEG_INF)
        return carry
    lax.fori_loop(0, n_tiles, idx_body, 0)

    _select_topk(keys_ref, selb_ref, ut_ref, n_tiles, topk)

    def softmax_pv(q_all, k_tile, v_tile, bias):
        m_ref[...] = jnp.full(m_ref.shape, NEG_INF, F32)

        def pass1(u, carry):
            lg = _dot_nt(q_all, k_tile(u))
            for h in range(N_HEADS):
                hs = slice(h * Q_BLOCK, (h + 1) * Q_BLOCK)
                lgh = lg[hs] + bias(u, h)
                lg_ref[u, hs, :] = lgh
                m_ref[hs, :] = jnp.maximum(m_ref[hs, :], jnp.maximum(lgh[:, :LANES], lgh[:, LANES:]))
            return carry
        lax.fori_loop(0, n_tiles, pass1, 0)
        m_ref[...] = jnp.broadcast_to(jnp.max(m_ref[...], axis=1, keepdims=True), m_ref.shape)
        l_ref[...] = jnp.zeros(l_ref.shape, F32)
        acc_ref[...] = jnp.zeros(acc_ref.shape, F32)

        def pass2(u, carry):
            m = m_ref[...]
            p_lo = jnp.exp(lg_ref[u, :, 0:LANES] - m)
            p_hi = jnp.exp(lg_ref[u, :, LANES:KEY_TILE] - m)
            l_ref[...] += p_lo + p_hi
            p = jnp.concatenate([p_lo, p_hi], axis=1).astype(MXU_DTYPE)
            acc_ref[...] += _dot(p, v_tile(u))
            return carry
        lax.fori_loop(0, n_tiles, pass2, 0)
        return acc_ref[...] / jnp.sum(l_ref[...], axis=1, keepdims=True)

    out_a = softmax_pv(qa_pad,
                       lambda u: kva_c[tile_rows(u), :],
                       lambda u: kva_c[tile_rows(u), :],
                       lambda u, h: selb_ref[u])
    for h in range(N_HEADS):
        o_ref[0, :, h * HEAD_DIM:(h + 1) * HEAD_DIM] = (
            out_a[h * Q_BLOCK:(h + 1) * Q_BLOCK, HEAD_DIM:2 * HEAD_DIM].astype(o_ref.dtype))

    out_b = softmax_pv(qb_bd,
                       lambda u: kvb_c[tile_rows(u), 0:2 * HEAD_DIM],
                       lambda u: kvb_c[tile_rows(u), 2 * HEAD_DIM:4 * HEAD_DIM],
                       lambda u, h: (cqb_ref[h] - ckt_c[u, h:h + 1, :]) + cmask_ref[u])
    per_group = N_HEADS // B_KV_HEADS
    for h in range(N_HEADS):
        g = h // per_group
        o_ref[0, :, (N_HEADS + h) * HEAD_DIM:(N_HEADS + h + 1) * HEAD_DIM] = (
            out_b[h * Q_BLOCK:(h + 1) * Q_BLOCK, g * HEAD_DIM:(g + 1) * HEAD_DIM].astype(o_ref.dtype))


def _prompt_attn(qcat, wc, kidx, kva, kvb, ckt, kidx_m, kva_m, kvb_m, ckt_m, n_meta, topk):
    nb, s, _ = qcat.shape
    assert s % Q_BLOCK == 0
    nq = s // Q_BLOCK
    n_u = (nq + 2) // 2
    kp = n_u * KEY_TILE
    rows = N_HEADS * Q_BLOCK
    ut = jnp.triu(jnp.ones((KEY_TILE, KEY_TILE), F32)).astype(MXU_DTYPE)
    blk = lambda b, j: (b, j, 0)
    per_b = lambda b, j: (b, 0, 0)
    const2 = lambda b, j: (0, 0)
    body = functools.partial(_prompt_attn_body, n_meta=n_meta, topk=topk)
    return pl.pallas_call(
        body,
        grid=(nb, nq),
        in_specs=[pl.BlockSpec((1, Q_BLOCK, qcat.shape[2]), blk),
                  pl.BlockSpec((1, Q_BLOCK, wc.shape[2]), blk),
                  pl.BlockSpec((1, s, kidx.shape[2]), per_b),
                  pl.BlockSpec((1, s, kva.shape[2]), per_b),
                  pl.BlockSpec((1, s, kvb.shape[2]), per_b),
                  pl.BlockSpec((1, N_HEADS, s), per_b),
                  pl.BlockSpec(kidx_m.shape, const2), pl.BlockSpec(kva_m.shape, const2),
                  pl.BlockSpec(kvb_m.shape, const2), pl.BlockSpec(ckt_m.shape, const2),
                  pl.BlockSpec(ut.shape, const2)],
        out_specs=pl.BlockSpec((1, Q_BLOCK, 2 * N_HEADS * HEAD_DIM), blk),
        out_shape=jax.ShapeDtypeStruct((nb, s, 2 * N_HEADS * HEAD_DIM), MXU_DTYPE),
        scratch_shapes=[pltpu.VMEM((kp, kidx.shape[2]), MXU_DTYPE),
                        pltpu.VMEM((kp, kva.shape[2]), MXU_DTYPE),
                        pltpu.VMEM((kp, kvb.shape[2]), MXU_DTYPE),
                        pltpu.VMEM((n_u, N_HEADS, KEY_TILE), F32),
                        pltpu.VMEM((n_u, Q_BLOCK, KEY_TILE), jnp.int32),
                        pltpu.VMEM((n_u, Q_BLOCK, KEY_TILE), F32),
                        pltpu.VMEM((n_u, Q_BLOCK, KEY_TILE), F32),
                        pltpu.VMEM((n_u, rows, KEY_TILE), F32),
                        pltpu.VMEM((rows, LANES), F32),
                        pltpu.VMEM((rows, LANES), F32),
                        pltpu.VMEM((rows, LANES), F32),
                        pltpu.VMEM((N_HEADS, Q_BLOCK, KEY_TILE), F32),
                        pltpu.VMEM((N_HEADS, Q_BLOCK, KEY_TILE), F32)],
        compiler_params=pltpu.CompilerParams(dimension_semantics=("arbitrary", "arbitrary"),
                                             vmem_limit_bytes=VMEM_LIMIT),
        name="prompt_attn",
    )(qcat, wc, kidx, kva, kvb, ckt, kidx_m, kva_m, kvb_m, ckt_m, ut)


def _page_copy(src_hbm, page, dst_buf, slot, p, rows_per_page, sem):
    return pltpu.make_async_copy(src_hbm.at[page],
                                 dst_buf.at[slot, pl.ds(p * rows_per_page, rows_per_page)], sem)


def _start_pages(pt_ref, sample, slot, n_pages, streams):
    def one(p, carry):
        page = pt_ref[sample * n_pages + p]
        for src, buf, rpp, sem in streams:
            _page_copy(src, page, buf, slot, p, rpp, sem.at[slot]).start()
        return carry
    lax.fori_loop(0, n_pages, one, 0)


def _wait_pages(slot, n_pages, streams):
    def one(p, carry):
        for src, buf, rpp, sem in streams:
            _page_copy(src, 0, buf, slot, p, rpp, sem.at[slot]).wait()
        return carry
    lax.fori_loop(0, n_pages, one, 0)


def _prefetch_schedule(pt_ref, n_pages, streams):
    b = pl.program_id(0)
    nb = pl.num_programs(0)
    slot = lax.rem(b, 2)

    @pl.when(b == 0)
    def _():
        _start_pages(pt_ref, 0, 0, n_pages, streams)

    @pl.when(b + 1 < nb)
    def _():
        _start_pages(pt_ref, b + 1, 1 - slot, n_pages, streams)

    _wait_pages(slot, n_pages, streams)
    return slot


SAMP_CHUNK = 1024


def _samp_scores_body(pt_ref, qi_ref, wi_ref, knew_ref, cache_hbm, out_ref, buf, sem, *, n_pages):
    slot = _prefetch_schedule(pt_ref, n_pages, [(cache_hbm, buf, PAGE, sem)])
    past = n_pages * PAGE
    qi = qi_ref[0]
    w = wi_ref[0]
    for c in range(past // SAMP_CHUNK):
        cs = slice(c * SAMP_CHUNK, (c + 1) * SAMP_CHUNK)
        kt = buf[slot, cs, :].astype(MXU_DTYPE)
        d = _dot_nt(qi, kt)
        out_ref[0, :, cs] = jnp.sum(w * jnp.maximum(d, 0.0), axis=0, keepdims=True)
    knew = knew_ref[0].astype(MXU_DTYPE).astype(F32)
    d_new = jnp.sum(qi.astype(F32) * knew, axis=1, keepdims=True)
    s_new = jnp.sum(w * jnp.maximum(d_new, 0.0), axis=0, keepdims=True)
    out_ref[0, :, past:past + LANES] = jnp.broadcast_to(s_new, (1, LANES))


def _samp_scores(page_table, qi16, wi16, k_new, cache_idx_k):
    db, n_pages = page_table.shape
    past = n_pages * PAGE
    assert past % SAMP_CHUNK == 0
    body = functools.partial(_samp_scores_body, n_pages=n_pages)
    row = lambda b, pt: (b, 0, 0)
    return pl.pallas_call(
        body,
        grid_spec=pltpu.PrefetchScalarGridSpec(
            num_scalar_prefetch=1, grid=(db,),
            in_specs=[pl.BlockSpec((1,) + qi16.shape[1:], row), pl.BlockSpec((1,) + wi16.shape[1:], row),
                      pl.BlockSpec((1, 1, HEAD_DIM), row), pl.BlockSpec(memory_space=pl.ANY)],
            out_specs=pl.BlockSpec((1, 1, past + LANES), row),
            scratch_shapes=[pltpu.VMEM((2, past, HEAD_DIM), F32), pltpu.SemaphoreType.DMA((2,))]),
        out_shape=jax.ShapeDtypeStruct((db, 1, past + LANES), F32),
        compiler_params=pltpu.CompilerParams(dimension_semantics=("arbitrary",), vmem_limit_bytes=VMEM_LIMIT),
        name="samp_scores",
    )(page_table.reshape(-1), qi16, wi16, k_new, cache_idx_k)


def _samp_select_body(sc_ref, ut_ref, selb_ref, keys_ref, *, n_pages, topk):
    n_t = keys_ref.shape[0]
    lane = lax.broadcasted_iota(jnp.int32, (Q_BLOCK, LANES), 1)
    for t in range(n_t):
        key = _sortable_key(sc_ref[t])
        if t == n_pages:
            key = jnp.where(lane == 0, key, INT_MIN)
        keys_ref[t] = key
    _select_topk(keys_ref, selb_ref, ut_ref, n_t, topk)


def _samp_select(scores_t, n_pages, topk):
    n_t, db, _ = scores_t.shape
    assert db % Q_BLOCK == 0 and n_t == n_pages + 1
    ut = jnp.triu(jnp.ones((LANES, LANES), F32)).astype(MXU_DTYPE)
    body = functools.partial(_samp_select_body, n_pages=n_pages, topk=topk)
    blk = pl.BlockSpec((n_t, Q_BLOCK, LANES), lambda r: (0, r, 0))
    return pl.pallas_call(
        body,
        grid=(db // Q_BLOCK,),
        in_specs=[blk, pl.BlockSpec(ut.shape, lambda r: (0, 0))],
        out_specs=blk,
        out_shape=jax.ShapeDtypeStruct(scores_t.shape, F32),
        scratch_shapes=[pltpu.VMEM((n_t, Q_BLOCK, LANES), jnp.int32)],
        compiler_params=pltpu.CompilerParams(dimension_semantics=("arbitrary",), vmem_limit_bytes=VMEM_LIMIT),
        name="samp_select",
    )(scores_t, ut)


def _samp_attn_body(pt_ref, qa_ref, qb_ref, bias_ref, anew_ref, bnew_ref, lfnew_ref, sl_ref,
                    akv_hbm, bkv_hbm, lft_hbm, oa_ref, ob_ref,
                    abuf, bbuf, lbuf, sem_a, sem_b, sem_l, abf, bbf, *, n_pages):
    slot = _prefetch_schedule(pt_ref, n_pages, [(akv_hbm, abuf, PAGE, sem_a), (bkv_hbm, bbuf, PAGE, sem_b),
                                                (lft_hbm, lbuf, N_HEADS, sem_l)])
    past = n_pages * PAGE
    n_chunks = past // SAMP_CHUNK
    pages_per_chunk = SAMP_CHUNK // PAGE

    def attend(q16, kv_bf, k_cols, v_cols, new_row, bias_chunk, bias_new):
        lgs = []
        m = jnp.full((2 * N_HEADS, 1), NEG_INF, F32)
        for c in range(n_chunks):
            cs = slice(c * SAMP_CHUNK, (c + 1) * SAMP_CHUNK)
            lg = _dot_nt(q16, kv_bf[cs, k_cols]) + bias_chunk(c)
            lgs.append(lg)
            m = jnp.maximum(m, jnp.max(lg, axis=1, keepdims=True))
        new_bf = new_row.astype(MXU_DTYPE).astype(F32)
        lg_new = jnp.sum(q16.astype(F32) * new_bf[:, k_cols], axis=1, keepdims=True) + bias_new
        m = jnp.maximum(m, lg_new)
        l = jnp.zeros((2 * N_HEADS, 1), F32)
        acc = jnp.zeros((2 * N_HEADS, LANES), F32)
        for c in range(n_chunks):
            cs = slice(c * SAMP_CHUNK, (c + 1) * SAMP_CHUNK)
            p = jnp.exp(lgs[c] - m)
            l = l + jnp.sum(p, axis=1, keepdims=True)
            acc = acc + _dot(p.astype(MXU_DTYPE), kv_bf[cs, v_cols])
        p_new = jnp.exp(lg_new - m)
        l = l + p_new
        acc = acc + p_new.astype(MXU_DTYPE).astype(F32) * new_bf[:, v_cols]
        return acc / l

    for c in range(n_chunks):
        cs = slice(c * SAMP_CHUNK, (c + 1) * SAMP_CHUNK)
        abf[cs, :] = abuf[slot, cs, :].astype(abf.dtype)
    all_lanes = slice(0, LANES)
    out_a = attend(qa_ref[0], abf, all_lanes, all_lanes, anew_ref[0],
                   lambda c: bias_ref[0, :, c * SAMP_CHUNK:(c + 1) * SAMP_CHUNK],
                   bias_ref[0, :, past:past + 1])
    oa_ref[0] = pltpu.roll(out_a, HEAD_DIM, 1)

    lf = lbuf[slot]
    sl = sl_ref[...]
    hi, mid, lo = _split3(lf)
    rev = _dot(hi, sl) + _dot(mid, sl) + _dot(lo, sl)
    tot = rev[:, 0:1] + lf[:, 0:1]
    run = jnp.zeros((N_HEADS, 1), F32)
    rp = [None] * n_pages
    for p in reversed(range(n_pages)):
        ps = slice(p * N_HEADS, (p + 1) * N_HEADS)
        rp[p] = rev[ps] + run
        run = run + tot[ps]
    nq = lfnew_ref[0]
    zpad = jnp.zeros((N_HEADS, SAMP_CHUNK), F32)

    def bias_b(c):
        rpc = jnp.concatenate(rp[c * pages_per_chunk:(c + 1) * pages_per_chunk], axis=1)
        return jnp.concatenate([rpc, zpad], axis=0) + nq

    for c in range(n_chunks):
        cs = slice(c * SAMP_CHUNK, (c + 1) * SAMP_CHUNK)
        bbf[cs, :] = bbuf[slot, cs, :].astype(bbf.dtype)
    out_b = attend(qb_ref[0], bbf, slice(0, 2 * HEAD_DIM), slice(2 * HEAD_DIM, 4 * HEAD_DIM), bnew_ref[0],
                   bias_b, jnp.zeros((2 * N_HEADS, 1), F32))
    head = lax.broadcasted_iota(jnp.int32, out_b.shape, 0)
    second_group = head >= N_HEADS // B_KV_HEADS
    ob_ref[0] = jnp.where(second_group, pltpu.roll(out_b, HEAD_DIM, 1), out_b)


def _samp_attn(page_table, qa16, qb16, bias, a_new, b_new, lf_new16, cache_a, cache_b, cache_lft):
    db, n_pages = page_table.shape
    past = n_pages * PAGE
    sl = jnp.tril(jnp.ones((LANES, LANES), F32), k=-1).astype(MXU_DTYPE)
    body = functools.partial(_samp_attn_body, n_pages=n_pages)
    row = lambda b, pt: (b, 0, 0)
    out_blk = pl.BlockSpec((1, 2 * N_HEADS, LANES), row)
    any_spec = pl.BlockSpec(memory_space=pl.ANY)
    return pl.pallas_call(
        body,
        grid_spec=pltpu.PrefetchScalarGridSpec(
            num_scalar_prefetch=1, grid=(db,),
            in_specs=[pl.BlockSpec((1,) + qa16.shape[1:], row), pl.BlockSpec((1,) + qb16.shape[1:], row),
                      pl.BlockSpec((1,) + bias.shape[1:], row), pl.BlockSpec((1,) + a_new.shape[1:], row),
                      pl.BlockSpec((1,) + b_new.shape[1:], row), pl.BlockSpec((1,) + lf_new16.shape[1:], row),
                      pl.BlockSpec(sl.shape, lambda b, pt: (0, 0)), any_spec, any_spec, any_spec],
            out_specs=[out_blk, out_blk],
            scratch_shapes=[pltpu.VMEM((2, past, 2 * HEAD_DIM), F32), pltpu.VMEM((2, past, 4 * HEAD_DIM), F32),
                            pltpu.VMEM((2, n_pages * N_HEADS, PAGE), F32),
                            pltpu.SemaphoreType.DMA((2,)), pltpu.SemaphoreType.DMA((2,)),
                            pltpu.SemaphoreType.DMA((2,)),
                            pltpu.VMEM((past, 2 * HEAD_DIM), MXU_DTYPE), pltpu.VMEM((past, 4 * HEAD_DIM), MXU_DTYPE)]),
        out_shape=[jax.ShapeDtypeStruct((db, 2 * N_HEADS, LANES), F32)] * 2,
        compiler_params=pltpu.CompilerParams(dimension_semantics=("arbitrary",), vmem_limit_bytes=VMEM_LIMIT),
        name="samp_attn",
    )(page_table.reshape(-1), qa16, qb16, bias, a_new, b_new, lf_new16, sl, cache_a, cache_b, cache_lft)


def _ffn_body(x_ref, o_ref, ge_ref, be_ref, wo_ref, g1_ref, b1_ref, wg_ref, wu_ref, wd_ref, g2_ref, b2_ref,
              y_ref, *, alpha, ff_chunk):
    h = _layer_norm(x_ref[...], ge_ref[...], be_ref[...])
    mixed = _dot(o_ref[...], wo_ref[...])
    h1 = _layer_norm(alpha * h + mixed, g1_ref[...], b1_ref[...])
    h1b = h1.astype(MXU_DTYPE)
    d_ff = wg_ref.shape[1]
    ffn = jnp.zeros(h1.shape, F32)
    for c in range(d_ff // ff_chunk):
        cs = slice(c * ff_chunk, (c + 1) * ff_chunk)
        gate = _dot(h1b, wg_ref[:, cs])
        up = _dot(h1b, wu_ref[:, cs])
        act = (gate * jax.nn.sigmoid(gate)) * up
        ffn = ffn + _dot(act.astype(MXU_DTYPE), wd_ref[cs, :])
    y_ref[...] = _layer_norm(alpha * h1 + ffn, g2_ref[...], b2_ref[...])


def _ffn_chunk(d_ff):
    for n in (11, 8, 6, 4, 2, 1):
        if d_ff % (n * LANES) == 0:
            return n * LANES
    raise ValueError(f"d_ff={d_ff} is not a multiple of {LANES}")


def _ffn_call(x2, o2, ln_e, w_o, ln1, w_gate, w_up, w_down, ln2, alpha, tm):
    m, d = x2.shape
    assert m % tm == 0
    d_ff = w_gate.shape[1]
    row = lambda i: (i, 0)
    const = lambda i: (0, 0)
    vec = pl.BlockSpec((1, d), const)
    resident = lambda shape: pl.BlockSpec(shape, const, pipeline_mode=pl.Buffered(1))
    body = functools.partial(_ffn_body, alpha=alpha, ff_chunk=_ffn_chunk(d_ff))
    return pl.pallas_call(
        body,
        grid=(m // tm,),
        in_specs=[pl.BlockSpec((tm, d), row), pl.BlockSpec((tm, o2.shape[1]), row), vec, vec,
                  resident(w_o.shape), vec, vec, resident(w_gate.shape), resident(w_up.shape),
                  resident(w_down.shape), vec, vec],
        out_specs=pl.BlockSpec((tm, d), row),
        out_shape=jax.ShapeDtypeStruct((m, d), F32),
        compiler_params=pltpu.CompilerParams(dimension_semantics=("arbitrary",), vmem_limit_bytes=VMEM_LIMIT),
        name="ffn",
    )(x2, o2, ln_e[0].reshape(1, d), ln_e[1].reshape(1, d), w_o, ln1[0].reshape(1, d), ln1[1].reshape(1, d),
      w_gate, w_up, w_down, ln2[0].reshape(1, d), ln2[1].reshape(1, d))


def _row_tile(n, cap):
    t = min(n, cap)
    while n % t:
        t //= 2
    return t


def kernel(x_prompt, x_sample, cache_a_kv, cache_idx_k, cache_b_kv, cache_b_logf, page_table, meta,
           ln_emb_g, ln_emb_b, w_in, b_forget, w_o, ln1_g, ln1_b, w_gate, w_up, w_down, ln2_g, ln2_b):
    depth = w_in.shape[0]
    assert depth == 1, "single-layer trunk only"
    bsz, seq, d = x_prompt.shape
    db, ds, _ = x_sample.shape
    assert ds == 1, "one new token per decode sequence"
    n_meta = meta.shape[0]
    n_pages = page_table.shape[1]
    past = n_pages * PAGE
    alpha = (2.0 * depth) ** 0.25
    topk_p = min(TOPK_MAX, seq // 4)
    topk_s = min(TOPK_MAX, (past + ds) // 4)

    sizes = (512, 64, 64, 512, 64, 8, 512, 128, 128, 8)
    offs = [0]
    for sz in sizes:
        offs.append(offs[-1] + sz)
    w0 = w_in[0]
    part = lambda i: w0[:, offs[i]:offs[i + 1]]
    qa_w, ka_w, va_w, qi_w, ki_w, wi_w, qb_w, kb_w, vb_w, fl_w = [part(i) for i in range(10)]
    pad_w = jnp.zeros((d, _C_KVB - _C_KIW - HEAD_DIM - 2 * N_HEADS), w0.dtype)
    w_cat = jnp.concatenate([qa_w, qi_w, qb_w, ka_w, va_w, ki_w, wi_w, fl_w, pad_w, kb_w, vb_w],
                            axis=1).astype(MXU_DTYPE)
    zero_c = jnp.zeros((1, N_HEADS), F32)
    proj = functools.partial(_proj_call, w_cat=w_cat, ln_g=ln_emb_g, ln_b=ln_emb_b, b_forget=b_forget[0])

    (_, wc_m, kidx_m, kva_m, kvb_m, akv_m, idxk_m, bkv_m, lf_m) = proj(
        meta[None].astype(x_prompt.dtype), jnp.arange(n_meta), zero_c, tm=n_meta)
    c_meta = wc_m[0, :, N_HEADS:]
    (qcat, wc, kidx, kva, kvb, akv, idxk, bkv, lf) = proj(
        x_prompt, n_meta + jnp.arange(seq), c_meta[n_meta - 1:n_meta], tm=_row_tile(seq, 512))
    (qcat_s, wc_s, _, _, _, akv_s, idxk_s, bkv_s, lf_s) = proj(
        x_sample.reshape(1, db, d), jnp.full((db,), past, jnp.int32), zero_c, tm=_row_tile(db, 512))

    pad_rows = lambda t: jnp.pad(t[0], ((0, LANES - n_meta), (0, 0)))
    ckt = jnp.swapaxes(wc[:, :, N_HEADS:], 1, 2)
    ckt_m = jnp.pad(c_meta.T, ((0, 0), (0, LANES - n_meta)))
    o_prompt = _prompt_attn(qcat, wc, kidx, kva, kvb, ckt, pad_rows(kidx_m), pad_rows(kva_m), pad_rows(kvb_m),
                            ckt_m, n_meta, topk_p)

    qs = qcat_s[0].astype(F32).reshape(db, 3, N_HEADS, HEAD_DIM)
    zeros_h = jnp.zeros((db, N_HEADS, HEAD_DIM), F32)
    pad_heads = lambda t: jnp.pad(t, ((0, 0), (0, N_HEADS), (0, 0)))
    qa16 = pad_heads(jnp.concatenate([qs[:, 0], zeros_h], axis=2)).astype(MXU_DTYPE)
    qi16 = pad_heads(qs[:, 1]).astype(MXU_DTYPE)
    half = N_HEADS // B_KV_HEADS
    head_group = jnp.arange(N_HEADS)[:, None] // half
    lane_group = jnp.arange(2 * HEAD_DIM)[None, :] // HEAD_DIM
    qb16 = pad_heads(jnp.where(head_group == lane_group, jnp.concatenate([qs[:, 2], qs[:, 2]], axis=2),
                               0.0)).astype(MXU_DTYPE)
    wi16 = pad_heads(wc_s[0, :, :N_HEADS, None])
    lfn16 = pad_heads(lf_s[0][:, :, None])
    scores = _samp_scores(page_table, qi16, wi16, idxk_s[0][:, None, :], cache_idx_k[0])
    n_t = n_pages + 1
    scores_t = jnp.swapaxes(scores.reshape(db, n_t, LANES), 0, 1)
    selb_t = _samp_select(scores_t, n_pages, topk_s)
    bias = jnp.swapaxes(selb_t, 0, 1).reshape(db, 1, n_t * LANES)
    cache_lft = jnp.swapaxes(cache_b_logf[0], 1, 2)
    oa_s, ob_s = _samp_attn(page_table, qa16, qb16, bias, akv_s[0][:, None, :], bkv_s[0][:, None, :], lfn16,
                            cache_a_kv[0].reshape(-1, PAGE, 2 * HEAD_DIM),
                            cache_b_kv[0].reshape(-1, PAGE, 4 * HEAD_DIM), cache_lft)
    heads_flat = lambda t: t[:, :N_HEADS, :HEAD_DIM].reshape(db, N_HEADS * HEAD_DIM)
    o_sample = jnp.concatenate([heads_flat(oa_s), heads_flat(ob_s)], axis=1).astype(MXU_DTYPE)

    cast = lambda w: w[0].astype(MXU_DTYPE)
    ffn = functools.partial(_ffn_call, ln_e=(ln_emb_g, ln_emb_b), w_o=cast(w_o), ln1=(ln1_g[0], ln1_b[0]),
                            w_gate=cast(w_gate), w_up=cast(w_up), w_down=cast(w_down), ln2=(ln2_g[0], ln2_b[0]),
                            alpha=alpha)
    y_prompt = ffn(x_prompt.reshape(bsz * seq, d), o_prompt.reshape(bsz * seq, -1),
                   tm=_row_tile(bsz * seq, 512)).reshape(bsz, seq, d)
    y_sample = ffn(x_sample.reshape(db, d), o_sample, tm=_row_tile(db, 512)).reshape(db, ds, d)

    def with_meta(m_part, x_part, tail):
        m_b = jnp.broadcast_to(m_part, (bsz,) + m_part.shape[1:])
        return jnp.concatenate([m_b, x_part], axis=1).reshape((1, bsz, n_meta + seq) + tail)

    return (y_prompt, y_sample,
            with_meta(akv_m, akv, (2, 1, HEAD_DIM)),
            with_meta(idxk_m, idxk, (HEAD_DIM,)),
            with_meta(bkv_m, bkv, (2, B_KV_HEADS, HEAD_DIM)),
            with_meta(lf_m, lf, (N_HEADS,)),
            akv_s[0].reshape(1, db, ds, 2, 1, HEAD_DIM),
            idxk_s[0].reshape(1, db, ds, HEAD_DIM),
            bkv_s[0].reshape(1, db, ds, 2, B_KV_HEADS, HEAD_DIM),
            lf_s[0].reshape(1, db, ds, N_HEADS))
```

```python
import functools

import jax
import jax.numpy as jnp
from jax import lax
from jax.experimental import pallas as pl
from jax.experimental.pallas import tpu as pltpu

HEAD_DIM = 64
N_HEADS = 8
B_KV_HEADS = 2
ROPE_HALF = 8
ROPE_THETA = 500000.0
TOPK_MAX = 256
PAGE = 128
LN_EPS = 1e-5
Q_BLOCK = 128
KEY_TILE = 256
LANES = 128
INT_MIN = -2 ** 31
NEG_INF = float("-inf")
VMEM_LIMIT = 56 * 1024 * 1024

F32 = jnp.float32
BF16 = jnp.bfloat16
MXU_DTYPE = BF16

_NT = (((1,), (1,)), ((), ()))


def _dot(a, b):
    return jnp.dot(a, b, preferred_element_type=F32)


def _dot_nt(a, b):
    return lax.dot_general(a, b, _NT, preferred_element_type=F32)


def _layer_norm(x, g, b):
    mu = jnp.mean(x, axis=-1, keepdims=True)
    xc = x - mu
    var = jnp.mean(xc * xc, axis=-1, keepdims=True)
    return xc * lax.rsqrt(var + LN_EPS) * g + b


def _split3(x):
    hi = x.astype(MXU_DTYPE)
    r1 = x - hi.astype(F32)
    mid = r1.astype(MXU_DTYPE)
    lo = (r1 - mid.astype(F32)).astype(MXU_DTYPE)
    return hi, mid, lo


_C_QA, _C_QI, _C_QB, _C_KVA, _C_KIW, _C_KVB, _C_END = 0, 512, 1024, 1536, 1664, 1792, 2048


def _rope(v, cos, sa, sb):
    return v * cos + pltpu.roll(v, LANES - ROPE_HALF, 1) * sa + pltpu.roll(v, ROPE_HALF, 1) * sb


def _proj_body(x_ref, g_ref, b_ref, w_ref, bf_ref, cos1_ref, sa1_ref, sb1_ref, cos2_ref, sa2_ref, sb2_ref,
               tri_ref, c0_ref,
               q_ref, wc_ref, kidx_ref, kva_ref, kvb_ref, akv_ref, idxk_ref, bkv_ref, lf_ref,
               carry_ref):
    j = pl.program_id(1)
    tm = x_ref.shape[1]
    h = _layer_norm(x_ref[0], g_ref[...], b_ref[...])
    proj = _dot(h.astype(MXU_DTYPE), w_ref[...])

    cos1, sa1, sb1 = cos1_ref[...], sa1_ref[...], sb1_ref[...]
    scale = HEAD_DIM ** -0.5
    for g in range(_C_QB // LANES):
        seg = proj[:, g * LANES:(g + 1) * LANES]
        q_ref[0, :, g * LANES:(g + 1) * LANES] = (_rope(seg, cos1, sa1, sb1) * scale).astype(q_ref.dtype)
    q_ref[0, :, _C_QB:_C_KVA] = (proj[:, _C_QB:_C_KVA] * scale).astype(q_ref.dtype)

    cos2, sa2, sb2 = cos2_ref[...], sa2_ref[...], sb2_ref[...]
    kva = _rope(proj[:, _C_KVA:_C_KIW], cos2, sa2, sb2)
    akv_ref[0] = kva
    kva_ref[0] = kva.astype(kva_ref.dtype)
    kiw = _rope(proj[:, _C_KIW:_C_KVB], cos2, sa2, sb2)
    idxk_ref[0] = kiw[:, :HEAD_DIM]
    kidx_ref[0] = kiw[:, :HEAD_DIM].astype(kidx_ref.dtype)
    wi = kiw[:, HEAD_DIM:HEAD_DIM + N_HEADS] * (N_HEADS ** -0.5)
    z = kiw[:, HEAD_DIM + N_HEADS:HEAD_DIM + 2 * N_HEADS] + bf_ref[...]
    lf = jnp.minimum(z, 0.0) - jnp.log1p(jnp.exp(-jnp.abs(z)))
    lf_ref[0] = lf
    kvb = proj[:, _C_KVB:_C_END]
    bkv_ref[0] = kvb
    kvb_ref[0] = kvb.astype(kvb_ref.dtype)

    @pl.when(j == 0)
    def _():
        carry_ref[...] = c0_ref[...]

    tri = tri_ref[...]
    hi, mid, lo = _split3(lf)
    cs = _dot(tri, hi) + _dot(tri, mid) + _dot(tri, lo) + carry_ref[...]
    carry_ref[...] = cs[tm - 1:tm, :]
    wc_ref[0] = jnp.concatenate([wi, cs], axis=1)


def _rope_tables(pos):
    inv = ROPE_THETA ** (-jnp.arange(ROPE_HALF, dtype=F32) / ROPE_HALF)
    ang = pos.astype(F32)[:, None] * inv
    cos, sin = jnp.cos(ang), jnp.sin(ang)
    ones = jnp.ones((pos.shape[0], HEAD_DIM - 2 * ROPE_HALF), F32)
    zeros = jnp.zeros_like(ones)
    zh = jnp.zeros_like(sin)
    cos_h = jnp.concatenate([cos, cos, ones], 1)
    sa_h = jnp.concatenate([-sin, zh, zeros], 1)
    sb_h = jnp.concatenate([zh, sin, zeros], 1)
    one_h, zero_h = jnp.ones_like(cos_h), jnp.zeros_like(cos_h)
    t1 = [jnp.concatenate([t, t], 1) for t in (cos_h, sa_h, sb_h)]
    t2 = [jnp.concatenate([cos_h, one_h], 1), jnp.concatenate([sa_h, zero_h], 1),
          jnp.concatenate([sb_h, zero_h], 1)]
    return t1 + t2


def _proj_call(x3, pos, c0, w_cat, ln_g, ln_b, b_forget, tm):
    nb, s, d = x3.shape
    assert s % tm == 0
    tables = _rope_tables(pos)
    tri = jnp.tril(jnp.ones((tm, tm), F32)).astype(MXU_DTYPE)
    row = lambda b, j: (b, j, 0)
    const2 = lambda b, j: (0, 0)
    tab_spec = pl.BlockSpec((tm, LANES), lambda b, j: (j, 0))
    widths = (3 * 512, 2 * N_HEADS, HEAD_DIM, 2 * HEAD_DIM, 4 * HEAD_DIM, 2 * HEAD_DIM, HEAD_DIM, 4 * HEAD_DIM,
              N_HEADS)
    dtypes = (MXU_DTYPE, F32, MXU_DTYPE, MXU_DTYPE, MXU_DTYPE, F32, F32, F32, F32)
    return pl.pallas_call(
        _proj_body,
        grid=(nb, s // tm),
        in_specs=[pl.BlockSpec((1, tm, d), row),
                  pl.BlockSpec((1, d), const2), pl.BlockSpec((1, d), const2),
                  pl.BlockSpec(w_cat.shape, const2),
                  pl.BlockSpec((1, N_HEADS), const2)] + [tab_spec] * 6 +
                 [pl.BlockSpec((tm, tm), const2), pl.BlockSpec((1, N_HEADS), const2)],
        out_specs=[pl.BlockSpec((1, tm, w), row) for w in widths],
        out_shape=[jax.ShapeDtypeStruct((nb, s, w), dt) for w, dt in zip(widths, dtypes)],
        scratch_shapes=[pltpu.VMEM((1, N_HEADS), F32)],
        compiler_params=pltpu.CompilerParams(dimension_semantics=("arbitrary", "arbitrary"),
                                             vmem_limit_bytes=VMEM_LIMIT),
        name="proj",
    )(x3, ln_g.reshape(1, d), ln_b.reshape(1, d), w_cat, b_forget.reshape(1, N_HEADS), *tables, tri, c0)


def _sortable_key(s):
    bits = pltpu.bitcast(s, jnp.int32)
    return bits ^ ((bits >> 31) & jnp.int32(0x7FFFFFFF))


def _select_topk(keys_ref, selb_ref, ut_ref, n_tiles, k):
    width = keys_ref.shape[2]
    halves = width // LANES
    kf = float(k)

    def count(pred):
        def body(t, acc):
            kt = keys_ref[t]
            for hh in range(halves):
                acc = acc + jnp.where(pred(kt[:, hh * LANES:(hh + 1) * LANES]), 1.0, 0.0)
            return acc
        acc = lax.fori_loop(0, n_tiles, body, jnp.zeros((Q_BLOCK, LANES), F32))
        return jnp.sum(acc, axis=1, keepdims=True)

    def bcast(col):
        return jnp.broadcast_to(col, (Q_BLOCK, LANES))

    def n_unsettled(cnt):
        return jnp.sum(jnp.where(cnt != kf, 1.0, 0.0))

    floor_b = jnp.full((Q_BLOCK, LANES), INT_MIN + 1, jnp.int32)
    n_adm = count(lambda x: x >= floor_b)
    cur0 = jnp.full((Q_BLOCK, 1), INT_MIN, jnp.int32)
    cnt0 = jnp.where(n_adm <= kf, kf, n_adm)

    def cond(st):
        it, _, _, bad = st
        return jnp.logical_and(it < 32, bad > 0.0)

    def body(st):
        it, cur, cnt, _ = st
        cand = cur + jnp.left_shift(jnp.int32(1), 31 - it)
        cand_b = bcast(cand)
        c = count(lambda x: x >= cand_b)
        take = c >= kf
        cur = jnp.where(take, cand, cur)
        cnt = jnp.where(take, c, cnt)
        return it + 1, cur, cnt, n_unsettled(cnt)

    _, cur, cnt, _ = lax.while_loop(cond, body, (jnp.int32(0), cur0, cnt0, n_unsettled(cnt0)))
    has_ties = jnp.sum(jnp.where(cnt > kf, 1.0, 0.0)) > 0.0

    @pl.when(jnp.logical_not(has_ties))
    def _():
        thr_b = bcast(jnp.maximum(cur, INT_MIN + 1))

        def body(t, carry):
            kt = keys_ref[t]
            for hh in range(halves):
                sl = slice(hh * LANES, (hh + 1) * LANES)
                selb_ref[t, :, sl] = jnp.where(kt[:, sl] >= thr_b, 0.0, NEG_INF)
            return carry
        lax.fori_loop(0, n_tiles, body, 0)

    @pl.when(has_ties)
    def _():
        cur_b = bcast(cur)
        need = kf - count(lambda x: x > cur_b)
        cur_w = jnp.broadcast_to(cur, (Q_BLOCK, width))
        ut = ut_ref[...]

        def body(t, seen):
            kt = keys_ref[t]
            eqf = jnp.where(kt == cur_w, jnp.where(kt > INT_MIN, 1.0, 0.0), 0.0)
            rank = _dot(eqf.astype(MXU_DTYPE), ut) + seen
            keep_tie = jnp.where(rank <= need, eqf, 0.0)
            selb_ref[t] = jnp.where(kt > cur_w, 0.0, jnp.where(keep_tie > 0.0, 0.0, NEG_INF))
            return seen + jnp.sum(eqf, axis=1, keepdims=True)
        lax.fori_loop(0, n_tiles, body, jnp.zeros((Q_BLOCK, 1), F32))


def _prompt_attn_body(qcat_ref, wc_ref, kidx_ref, kva_ref, kvb_ref, ckt_ref,
                      kidxm_ref, kvam_ref, kvbm_ref, cktm_ref, ut_ref,
                      o_ref,
                      kidx_c, kva_c, kvb_c, ckt_c, keys_ref, cmask_ref, selb_ref, lg_ref, m_ref, l_ref, acc_ref,
                      wb_ref, cqb_ref, *, n_meta, topk):
    j = pl.program_id(1)
    s = kidx_ref.shape[1]
    kp = kidx_c.shape[0]
    n_u = keys_ref.shape[0]
    rows = N_HEADS * Q_BLOCK

    @pl.when(j == 0)
    def _():
        for dst, m_src, x_src in ((kidx_c, kidxm_ref, kidx_ref), (kva_c, kvam_ref, kva_ref),
                                  (kvb_c, kvbm_ref, kvb_ref)):
            dst[0:LANES, :] = m_src[...]
            dst[LANES:LANES + s, :] = x_src[0]
            if kp > LANES + s:
                dst[LANES + s:, :] = jnp.zeros((kp - LANES - s, dst.shape[1]), dst.dtype)
        ck = jnp.concatenate([cktm_ref[...], ckt_ref[0], jnp.zeros((N_HEADS, kp - LANES - s), F32)], axis=1)
        for u in range(n_u):
            ckt_c[u] = ck[:, u * KEY_TILE:(u + 1) * KEY_TILE]

    q = qcat_ref[0]

    def stack_heads(off):
        return jnp.concatenate([q[:, off + h * HEAD_DIM:off + (h + 1) * HEAD_DIM] for h in range(N_HEADS)], axis=0)

    qa_all, qi_all, qb_all = stack_heads(_C_QA), stack_heads(_C_QI), stack_heads(_C_QB)
    z_all = jnp.zeros((rows, HEAD_DIM), q.dtype)
    z_half = jnp.zeros((rows // 2, HEAD_DIM), q.dtype)
    qa_pad = jnp.concatenate([qa_all, z_all], axis=1)
    qb_bd = jnp.concatenate([jnp.concatenate([qb_all[:rows // 2], z_half], axis=1),
                             jnp.concatenate([z_half, qb_all[rows // 2:]], axis=1)], axis=0)
    wcv = wc_ref[0]
    for h in range(N_HEADS):
        wb_ref[h] = jnp.broadcast_to(wcv[:, h:h + 1], (Q_BLOCK, KEY_TILE))
        cqb_ref[h] = jnp.broadcast_to(wcv[:, N_HEADS + h:N_HEADS + h + 1], (Q_BLOCK, KEY_TILE))

    n_tiles = jnp.right_shift(j + 3, 1)

    def tile_rows(u):
        return pl.ds(pl.multiple_of(u * KEY_TILE, KEY_TILE), KEY_TILE)

    def idx_body(u, carry):
        d = _dot_nt(qi_all, kidx_c[tile_rows(u), :])
        sc = jnp.zeros((Q_BLOCK, KEY_TILE), F32)
        for h in range(N_HEADS):
            sc = sc + wb_ref[h] * jnp.maximum(d[h * Q_BLOCK:(h + 1) * Q_BLOCK], 0.0)
        col = u * KEY_TILE + lax.broadcasted_iota(jnp.int32, (Q_BLOCK, KEY_TILE), 1)
        row = lax.broadcasted_iota(jnp.int32, (Q_BLOCK, KEY_TILE), 0)
        valid = jnp.logical_or(col < n_meta,
                               jnp.logical_and(col >= LANES, col - LANES <= j * Q_BLOCK + row))
        keys_ref[u] = jnp.where(valid, _sortable_key(sc), INT_MIN)
        cmask_ref[u] = jnp.where(valid, 0.0, NEG_INF)
        return carry
    lax.fori_loop(0, n_tiles, idx_body, 0)

    _select_topk(keys_ref, selb_ref, ut_ref, n_tiles, topk)

    def softmax_pv(q_all, k_tile, v_tile, bias):
        m_ref[...] = jnp.full(m_ref.shape, NEG_INF, F32)

        def pass1(u, carry):
            lg = _dot_nt(q_all, k_tile(u))
            for h in range(N_HEADS):
                hs = slice(h * Q_BLOCK, (h + 1) * Q_BLOCK)
                lgh = lg[hs] + bias(u, h)
                lg_ref[u, hs, :] = lgh
                m_ref[hs, :] = jnp.maximum(m_ref[hs, :], jnp.maximum(lgh[:, :LANES], lgh[:, LANES:]))
            return carry
        lax.fori_loop(0, n_tiles, pass1, 0)
        m_ref[...] = jnp.broadcast_to(jnp.max(m_ref[...], axis=1, keepdims=True), m_ref.shape)
        l_ref[...] = jnp.zeros(l_ref.shape, F32)
        acc_ref[...] = jnp.zeros(acc_ref.shape, F32)

        def pass2(u, carry):
            m = m_ref[...]
            p_lo = jnp.exp(lg_ref[u, :, 0:LANES] - m)
            p_hi = jnp.exp(lg_ref[u, :, LANES:KEY_TILE] - m)
            l_ref[...] += p_lo + p_hi
            p = jnp.concatenate([p_lo, p_hi], axis=1).astype(MXU_DTYPE)
            acc_ref[...] += _dot(p, v_tile(u))
            return carry
        lax.fori_loop(0, n_tiles, pass2, 0)
        return acc_ref[...] / jnp.sum(l_ref[...], axis=1, keepdims=True)

    out_a = softmax_pv(qa_pad,
                       lambda u: kva_c[tile_rows(u), :],
                       lambda u: kva_c[tile_rows(u), :],
                       lambda u, h: selb_ref[u])
    for h in range(N_HEADS):
        o_ref[0, :, h * HEAD_DIM:(h + 1) * HEAD_DIM] = (
            out_a[h * Q_BLOCK:(h + 1) * Q_BLOCK, HEAD_DIM:2 * HEAD_DIM].astype(o_ref.dtype))

    out_b = softmax_pv(qb_bd,
                       lambda u: kvb_c[tile_rows(u), 0:2 * HEAD_DIM],
                       lambda u: kvb_c[tile_rows(u), 2 * HEAD_DIM:4 * HEAD_DIM],
                       lambda u, h: (cqb_ref[h] - ckt_c[u, h:h + 1, :]) + cmask_ref[u])
    per_group = N_HEADS // B_KV_HEADS
    for h in range(N_HEADS):
        g = h // per_group
        o_ref[0, :, (N_HEADS + h) * HEAD_DIM:(N_HEADS + h + 1) * HEAD_DIM] = (
            out_b[h * Q_BLOCK:(h + 1) * Q_BLOCK, g * HEAD_DIM:(g + 1) * HEAD_DIM].astype(o_ref.dtype))


def _prompt_attn(qcat, wc, kidx, kva, kvb, ckt, kidx_m, kva_m, kvb_m, ckt_m, n_meta, topk):
    nb, s, _ = qcat.shape
    assert s % Q_BLOCK == 0
    nq = s // Q_BLOCK
    n_u = (nq + 2) // 2
    kp = n_u * KEY_TILE
    rows = N_HEADS * Q_BLOCK
    ut = jnp.triu(jnp.ones((KEY_TILE, KEY_TILE), F32)).astype(MXU_DTYPE)
    blk = lambda b, j: (b, j, 0)
    per_b = lambda b, j: (b, 0, 0)
    const2 = lambda b, j: (0, 0)
    body = functools.partial(_prompt_attn_body, n_meta=n_meta, topk=topk)
    return pl.pallas_call(
        body,
        grid=(nb, nq),
        in_specs=[pl.BlockSpec((1, Q_BLOCK, qcat.shape[2]), blk),
                  pl.BlockSpec((1, Q_BLOCK, wc.shape[2]), blk),
                  pl.BlockSpec((1, s, kidx.shape[2]), per_b),
                  pl.BlockSpec((1, s, kva.shape[2]), per_b),
                  pl.BlockSpec((1, s, kvb.shape[2]), per_b),
                  pl.BlockSpec((1, N_HEADS, s), per_b),
                  pl.BlockSpec(kidx_m.shape, const2), pl.BlockSpec(kva_m.shape, const2),
                  pl.BlockSpec(kvb_m.shape, const2), pl.BlockSpec(ckt_m.shape, const2),
                  pl.BlockSpec(ut.shape, const2)],
        out_specs=pl.BlockSpec((1, Q_BLOCK, 2 * N_HEADS * HEAD_DIM), blk),
        out_shape=jax.ShapeDtypeStruct((nb, s, 2 * N_HEADS * HEAD_DIM), MXU_DTYPE),
        scratch_shapes=[pltpu.VMEM((kp, kidx.shape[2]), MXU_DTYPE),
                        pltpu.VMEM((kp, kva.shape[2]), MXU_DTYPE),
                        pltpu.VMEM((kp, kvb.shape[2]), MXU_DTYPE),
                        pltpu.VMEM((n_u, N_HEADS, KEY_TILE), F32),
                        pltpu.VMEM((n_u, Q_BLOCK, KEY_TILE), jnp.int32),
                        pltpu.VMEM((n_u, Q_BLOCK, KEY_TILE), F32),
                        pltpu.VMEM((n_u, Q_BLOCK, KEY_TILE), F32),
                        pltpu.VMEM((n_u, rows, KEY_TILE), F32),
                        pltpu.VMEM((rows, LANES), F32),
                        pltpu.VMEM((rows, LANES), F32),
                        pltpu.VMEM((rows, LANES), F32),
                        pltpu.VMEM((N_HEADS, Q_BLOCK, KEY_TILE), F32),
                        pltpu.VMEM((N_HEADS, Q_BLOCK, KEY_TILE), F32)],
        compiler_params=pltpu.CompilerParams(dimension_semantics=("arbitrary", "arbitrary"),
                                             vmem_limit_bytes=VMEM_LIMIT),
        name="prompt_attn",
    )(qcat, wc, kidx, kva, kvb, ckt, kidx_m, kva_m, kvb_m, ckt_m, ut)


def _lane_window(buf, slot, p):
    return buf.at[slot, :, pl.ds(pl.multiple_of(p * PAGE, PAGE), PAGE)]


def _row_window(buf, slot, p):
    rows = N_HEADS
    return buf.at[slot, pl.ds(pl.multiple_of(p * rows, rows), rows), :]


def _page_copy(stream, page, slot, p):
    src_hbm, buf, window, sem = stream
    return pltpu.make_async_copy(src_hbm.at[page], window(buf, slot, p), sem.at[slot])


def _start_pages(pt_ref, sample, slot, n_pages, streams):
    def one(p, carry):
        page = pt_ref[sample * n_pages + p]
        for stream in streams:
            _page_copy(stream, page, slot, p).start()
        return carry
    lax.fori_loop(0, n_pages, one, 0)


def _wait_pages(slot, n_pages, streams):
    def one(p, carry):
        for stream in streams:
            _page_copy(stream, 0, slot, p).wait()
        return carry
    lax.fori_loop(0, n_pages, one, 0)


def _prefetch_schedule(pt_ref, n_pages, streams):
    b = pl.program_id(0)
    nb = pl.num_programs(0)
    slot = lax.rem(b, 2)

    @pl.when(b == 0)
    def _():
        _start_pages(pt_ref, 0, 0, n_pages, streams)

    @pl.when(b + 1 < nb)
    def _():
        _start_pages(pt_ref, b + 1, 1 - slot, n_pages, streams)

    _wait_pages(slot, n_pages, streams)
    return slot


SAMP_CHUNK = 1024


def _samp_scores_body(pt_ref, qi_ref, wi_ref, knew_ref, cache_hbm, out_ref, buf, sem, *, n_pages):
    slot = _prefetch_schedule(pt_ref, n_pages, [(cache_hbm, buf, _lane_window, sem)])
    past = n_pages * PAGE
    qi = qi_ref[0]
    w = wi_ref[0]
    for c in range(past // SAMP_CHUNK):
        cs = slice(c * SAMP_CHUNK, (c + 1) * SAMP_CHUNK)
        kt = buf[slot, :, cs].astype(MXU_DTYPE)
        d = _dot(qi, kt)
        out_ref[0, :, cs] = jnp.sum(w * jnp.maximum(d, 0.0), axis=0, keepdims=True)
    knew = knew_ref[0].astype(MXU_DTYPE).astype(F32)
    d_new = jnp.sum(qi.astype(F32) * knew, axis=1, keepdims=True)
    s_new = jnp.sum(w * jnp.maximum(d_new, 0.0), axis=0, keepdims=True)
    out_ref[0, :, past:past + LANES] = jnp.broadcast_to(s_new, (1, LANES))


def _samp_scores(page_table, qi16, wi16, k_new, cache_idx_k):
    db, n_pages = page_table.shape
    past = n_pages * PAGE
    assert past % SAMP_CHUNK == 0
    body = functools.partial(_samp_scores_body, n_pages=n_pages)
    row = lambda b, pt: (b, 0, 0)
    return pl.pallas_call(
        body,
        grid_spec=pltpu.PrefetchScalarGridSpec(
            num_scalar_prefetch=1, grid=(db,),
            in_specs=[pl.BlockSpec((1,) + qi16.shape[1:], row), pl.BlockSpec((1,) + wi16.shape[1:], row),
                      pl.BlockSpec((1, 1, HEAD_DIM), row), pl.BlockSpec(memory_space=pl.ANY)],
            out_specs=pl.BlockSpec((1, 1, past + LANES), row),
            scratch_shapes=[pltpu.VMEM((2, HEAD_DIM, past), F32), pltpu.SemaphoreType.DMA((2,))]),
        out_shape=jax.ShapeDtypeStruct((db, 1, past + LANES), F32),
        compiler_params=pltpu.CompilerParams(dimension_semantics=("arbitrary",), vmem_limit_bytes=VMEM_LIMIT),
        name="samp_scores",
    )(page_table.reshape(-1), qi16, wi16, k_new, cache_idx_k)


def _samp_select_body(sc_ref, ut_ref, selb_ref, keys_ref, *, n_pages, topk):
    n_t = keys_ref.shape[0]
    lane = lax.broadcasted_iota(jnp.int32, (Q_BLOCK, LANES), 1)
    for t in range(n_t):
        key = _sortable_key(sc_ref[t])
        if t == n_pages:
            key = jnp.where(lane == 0, key, INT_MIN)
        keys_ref[t] = key
    _select_topk(keys_ref, selb_ref, ut_ref, n_t, topk)


def _samp_select(scores_t, n_pages, topk):
    n_t, db, _ = scores_t.shape
    assert db % Q_BLOCK == 0 and n_t == n_pages + 1
    ut = jnp.triu(jnp.ones((LANES, LANES), F32)).astype(MXU_DTYPE)
    body = functools.partial(_samp_select_body, n_pages=n_pages, topk=topk)
    blk = pl.BlockSpec((n_t, Q_BLOCK, LANES), lambda r: (0, r, 0))
    return pl.pallas_call(
        body,
        grid=(db // Q_BLOCK,),
        in_specs=[blk, pl.BlockSpec(ut.shape, lambda r: (0, 0))],
        out_specs=blk,
        out_shape=jax.ShapeDtypeStruct(scores_t.shape, F32),
        scratch_shapes=[pltpu.VMEM((n_t, Q_BLOCK, LANES), jnp.int32)],
        compiler_params=pltpu.CompilerParams(dimension_semantics=("arbitrary",), vmem_limit_bytes=VMEM_LIMIT),
        name="samp_select",
    )(scores_t, ut)


def _samp_attn_body(pt_ref, qa_ref, qb_ref, bias_ref, anew_ref, bnew_ref, lfnew_ref, sl_ref,
                    akv_hbm, bkv_hbm, lft_hbm, oa_ref, ob_ref,
                    abuf, bbuf, lbuf, sem_a, sem_b, sem_l, abf, bbf, *, n_pages):
    slot = _prefetch_schedule(pt_ref, n_pages, [(akv_hbm, abuf, _lane_window, sem_a),
                                                (bkv_hbm, bbuf, _lane_window, sem_b),
                                                (lft_hbm, lbuf, _row_window, sem_l)])
    past = n_pages * PAGE
    n_chunks = past // SAMP_CHUNK
    pages_per_chunk = SAMP_CHUNK // PAGE

    def attend(q16, kv_bf, n_k, new_row, bias_chunk, bias_new):
        k_rows, v_rows = slice(0, n_k), slice(n_k, 2 * n_k)
        lgs = []
        m = jnp.full((2 * N_HEADS, 1), NEG_INF, F32)
        for c in range(n_chunks):
            cs = slice(c * SAMP_CHUNK, (c + 1) * SAMP_CHUNK)
            lg = _dot(q16, kv_bf[k_rows, cs]) + bias_chunk(c)
            lgs.append(lg)
            m = jnp.maximum(m, jnp.max(lg, axis=1, keepdims=True))
        new_bf = new_row.astype(MXU_DTYPE).astype(F32)
        lg_new = jnp.sum(q16.astype(F32) * new_bf[:, k_rows], axis=1, keepdims=True) + bias_new
        m = jnp.maximum(m, lg_new)
        l = jnp.zeros((2 * N_HEADS, 1), F32)
        acc = jnp.zeros((2 * N_HEADS, n_k), F32)
        for c in range(n_chunks):
            cs = slice(c * SAMP_CHUNK, (c + 1) * SAMP_CHUNK)
            p = jnp.exp(lgs[c] - m)
            l = l + jnp.sum(p, axis=1, keepdims=True)
            acc = acc + _dot_nt(p.astype(MXU_DTYPE), kv_bf[v_rows, cs])
        p_new = jnp.exp(lg_new - m)
        l = l + p_new
        acc = acc + p_new.astype(MXU_DTYPE).astype(F32) * new_bf[:, v_rows]
        return acc / l

    for c in range(n_chunks):
        cs = slice(c * SAMP_CHUNK, (c + 1) * SAMP_CHUNK)
        abf[:, cs] = abuf[slot, :, cs].astype(abf.dtype)
    oa_ref[0] = attend(qa_ref[0], abf, HEAD_DIM, anew_ref[0],
                       lambda c: bias_ref[0, :, c * SAMP_CHUNK:(c + 1) * SAMP_CHUNK],
                       bias_ref[0, :, past:past + 1])

    lf = lbuf[slot]
    sl = sl_ref[...]
    hi, mid, lo = _split3(lf)
    rev = _dot(hi, sl) + _dot(mid, sl) + _dot(lo, sl)
    tot = rev[:, 0:1] + lf[:, 0:1]
    run = jnp.zeros((N_HEADS, 1), F32)
    rp = [None] * n_pages
    for p in reversed(range(n_pages)):
        ps = slice(p * N_HEADS, (p + 1) * N_HEADS)
        rp[p] = rev[ps] + run
        run = run + tot[ps]
    nq = lfnew_ref[0]
    zpad = jnp.zeros((N_HEADS, SAMP_CHUNK), F32)

    def bias_b(c):
        rpc = jnp.concatenate(rp[c * pages_per_chunk:(c + 1) * pages_per_chunk], axis=1)
        return jnp.concatenate([rpc, zpad], axis=0) + nq

    for c in range(n_chunks):
        cs = slice(c * SAMP_CHUNK, (c + 1) * SAMP_CHUNK)
        bbf[:, cs] = bbuf[slot, :, cs].astype(bbf.dtype)
    out_b = attend(qb_ref[0], bbf, 2 * HEAD_DIM, bnew_ref[0], bias_b, jnp.zeros((2 * N_HEADS, 1), F32))
    head = lax.broadcasted_iota(jnp.int32, out_b.shape, 0)
    second_group = head >= N_HEADS // B_KV_HEADS
    ob_ref[0] = jnp.where(second_group, pltpu.roll(out_b, HEAD_DIM, 1), out_b)


def _samp_attn(page_table, qa16, qb16, bias, a_new, b_new, lf_new16, cache_a, cache_b, cache_lft):
    db, n_pages = page_table.shape
    past = n_pages * PAGE
    sl = jnp.tril(jnp.ones((LANES, LANES), F32), k=-1).astype(MXU_DTYPE)
    body = functools.partial(_samp_attn_body, n_pages=n_pages)
    row = lambda b, pt: (b, 0, 0)
    any_spec = pl.BlockSpec(memory_space=pl.ANY)
    return pl.pallas_call(
        body,
        grid_spec=pltpu.PrefetchScalarGridSpec(
            num_scalar_prefetch=1, grid=(db,),
            in_specs=[pl.BlockSpec((1,) + qa16.shape[1:], row), pl.BlockSpec((1,) + qb16.shape[1:], row),
                      pl.BlockSpec((1,) + bias.shape[1:], row), pl.BlockSpec((1,) + a_new.shape[1:], row),
                      pl.BlockSpec((1,) + b_new.shape[1:], row), pl.BlockSpec((1,) + lf_new16.shape[1:], row),
                      pl.BlockSpec(sl.shape, lambda b, pt: (0, 0)), any_spec, any_spec, any_spec],
            out_specs=[pl.BlockSpec((1, 2 * N_HEADS, HEAD_DIM), row),
                       pl.BlockSpec((1, 2 * N_HEADS, 2 * HEAD_DIM), row)],
            scratch_shapes=[pltpu.VMEM((2, 2 * HEAD_DIM, past), F32), pltpu.VMEM((2, 4 * HEAD_DIM, past), F32),
                            pltpu.VMEM((2, n_pages * N_HEADS, PAGE), F32),
                            pltpu.SemaphoreType.DMA((2,)), pltpu.SemaphoreType.DMA((2,)),
                            pltpu.SemaphoreType.DMA((2,)),
                            pltpu.VMEM((2 * HEAD_DIM, past), MXU_DTYPE), pltpu.VMEM((4 * HEAD_DIM, past), MXU_DTYPE)]),
        out_shape=[jax.ShapeDtypeStruct((db, 2 * N_HEADS, HEAD_DIM), F32),
                   jax.ShapeDtypeStruct((db, 2 * N_HEADS, 2 * HEAD_DIM), F32)],
        compiler_params=pltpu.CompilerParams(dimension_semantics=("arbitrary",), vmem_limit_bytes=VMEM_LIMIT),
        name="samp_attn",
    )(page_table.reshape(-1), qa16, qb16, bias, a_new, b_new, lf_new16, sl, cache_a, cache_b, cache_lft)


def _ffn_body(x_ref, o_ref, ge_ref, be_ref, wo_ref, g1_ref, b1_ref, wg_ref, wu_ref, wd_ref, g2_ref, b2_ref,
              y_ref, *, alpha, ff_chunk):
    h = _layer_norm(x_ref[...], ge_ref[...], be_ref[...])
    mixed = _dot(o_ref[...], wo_ref[...])
    h1 = _layer_norm(alpha * h + mixed, g1_ref[...], b1_ref[...])
    h1b = h1.astype(MXU_DTYPE)
    d_ff = wg_ref.shape[1]
    ffn = jnp.zeros(h1.shape, F32)
    for c in range(d_ff // ff_chunk):
        cs = slice(c * ff_chunk, (c + 1) * ff_chunk)
        gate = _dot(h1b, wg_ref[:, cs])
        up = _dot(h1b, wu_ref[:, cs])
        act = (gate * jax.nn.sigmoid(gate)) * up
        ffn = ffn + _dot(act.astype(MXU_DTYPE), wd_ref[cs, :])
    y_ref[...] = _layer_norm(alpha * h1 + ffn, g2_ref[...], b2_ref[...])


def _ffn_chunk(d_ff):
    for n in (11, 8, 6, 4, 2, 1):
        if d_ff % (n * LANES) == 0:
            return n * LANES
    raise ValueError(f"d_ff={d_ff} is not a multiple of {LANES}")


def _ffn_call(x2, o2, ln_e, w_o, ln1, w_gate, w_up, w_down, ln2, alpha, tm):
    m, d = x2.shape
    assert m % tm == 0
    d_ff = w_gate.shape[1]
    row = lambda i: (i, 0)
    const = lambda i: (0, 0)
    vec = pl.BlockSpec((1, d), const)
    resident = lambda shape: pl.BlockSpec(shape, const, pipeline_mode=pl.Buffered(1))
    body = functools.partial(_ffn_body, alpha=alpha, ff_chunk=_ffn_chunk(d_ff))
    return pl.pallas_call(
        body,
        grid=(m // tm,),
        in_specs=[pl.BlockSpec((tm, d), row), pl.BlockSpec((tm, o2.shape[1]), row), vec, vec,
                  resident(w_o.shape), vec, vec, resident(w_gate.shape), resident(w_up.shape),
                  resident(w_down.shape), vec, vec],
        out_specs=pl.BlockSpec((tm, d), row),
        out_shape=jax.ShapeDtypeStruct((m, d), F32),
        compiler_params=pltpu.CompilerParams(dimension_semantics=("arbitrary",), vmem_limit_bytes=VMEM_LIMIT),
        name="ffn",
    )(x2, o2, ln_e[0].reshape(1, d), ln_e[1].reshape(1, d), w_o, ln1[0].reshape(1, d), ln1[1].reshape(1, d),
      w_gate, w_up, w_down, ln2[0].reshape(1, d), ln2[1].reshape(1, d))


def _row_tile(n, cap):
    t = min(n, cap)
    while n % t:
        t //= 2
    return t


def kernel(x_prompt, x_sample, cache_a_kv, cache_idx_k, cache_b_kv, cache_b_logf, page_table, meta,
           ln_emb_g, ln_emb_b, w_in, b_forget, w_o, ln1_g, ln1_b, w_gate, w_up, w_down, ln2_g, ln2_b):
    depth = w_in.shape[0]
    assert depth == 1, "single-layer trunk only"
    bsz, seq, d = x_prompt.shape
    db, ds, _ = x_sample.shape
    assert ds == 1, "one new token per decode sequence"
    n_meta = meta.shape[0]
    n_pages = page_table.shape[1]
    past = n_pages * PAGE
    alpha = (2.0 * depth) ** 0.25
    topk_p = min(TOPK_MAX, seq // 4)
    topk_s = min(TOPK_MAX, (past + ds) // 4)

    sizes = (512, 64, 64, 512, 64, 8, 512, 128, 128, 8)
    offs = [0]
    for sz in sizes:
        offs.append(offs[-1] + sz)
    w0 = w_in[0]
    part = lambda i: w0[:, offs[i]:offs[i + 1]]
    qa_w, ka_w, va_w, qi_w, ki_w, wi_w, qb_w, kb_w, vb_w, fl_w = [part(i) for i in range(10)]
    pad_w = jnp.zeros((d, _C_KVB - _C_KIW - HEAD_DIM - 2 * N_HEADS), w0.dtype)
    w_cat = jnp.concatenate([qa_w, qi_w, qb_w, ka_w, va_w, ki_w, wi_w, fl_w, pad_w, kb_w, vb_w],
                            axis=1).astype(MXU_DTYPE)
    zero_c = jnp.zeros((1, N_HEADS), F32)
    proj = functools.partial(_proj_call, w_cat=w_cat, ln_g=ln_emb_g, ln_b=ln_emb_b, b_forget=b_forget[0])

    (_, wc_m, kidx_m, kva_m, kvb_m, akv_m, idxk_m, bkv_m, lf_m) = proj(
        meta[None].astype(x_prompt.dtype), jnp.arange(n_meta), zero_c, tm=n_meta)
    c_meta = wc_m[0, :, N_HEADS:]
    (qcat, wc, kidx, kva, kvb, akv, idxk, bkv, lf) = proj(
        x_prompt, n_meta + jnp.arange(seq), c_meta[n_meta - 1:n_meta], tm=_row_tile(seq, 512))
    (qcat_s, wc_s, _, _, _, akv_s, idxk_s, bkv_s, lf_s) = proj(
        x_sample.reshape(1, db, d), jnp.full((db,), past, jnp.int32), zero_c, tm=_row_tile(db, 512))

    pad_rows = lambda t: jnp.pad(t[0], ((0, LANES - n_meta), (0, 0)))
    ckt = jnp.swapaxes(wc[:, :, N_HEADS:], 1, 2)
    ckt_m = jnp.pad(c_meta.T, ((0, 0), (0, LANES - n_meta)))
    o_prompt = _prompt_attn(qcat, wc, kidx, kva, kvb, ckt, pad_rows(kidx_m), pad_rows(kva_m), pad_rows(kvb_m),
                            ckt_m, n_meta, topk_p)

    qs = qcat_s[0].astype(F32).reshape(db, 3, N_HEADS, HEAD_DIM)
    zeros_h = jnp.zeros((db, N_HEADS, HEAD_DIM), F32)
    pad_heads = lambda t: jnp.pad(t, ((0, 0), (0, N_HEADS), (0, 0)))
    qa16 = pad_heads(qs[:, 0]).astype(MXU_DTYPE)
    qi16 = pad_heads(qs[:, 1]).astype(MXU_DTYPE)
    half = N_HEADS // B_KV_HEADS
    head_group = jnp.arange(N_HEADS)[:, None] // half
    lane_group = jnp.arange(2 * HEAD_DIM)[None, :] // HEAD_DIM
    qb16 = pad_heads(jnp.where(head_group == lane_group, jnp.concatenate([qs[:, 2], qs[:, 2]], axis=2),
                               0.0)).astype(MXU_DTYPE)
    wi16 = pad_heads(wc_s[0, :, :N_HEADS, None])
    lfn16 = pad_heads(lf_s[0][:, :, None])
    feat_major = lambda c: jnp.moveaxis(c[0], 1, -1).reshape(c.shape[1], -1, PAGE)
    scores = _samp_scores(page_table, qi16, wi16, idxk_s[0][:, None, :], feat_major(cache_idx_k))
    n_t = n_pages + 1
    scores_t = jnp.swapaxes(scores.reshape(db, n_t, LANES), 0, 1)
    selb_t = _samp_select(scores_t, n_pages, topk_s)
    bias = jnp.swapaxes(selb_t, 0, 1).reshape(db, 1, n_t * LANES)
    oa_s, ob_s = _samp_attn(page_table, qa16, qb16, bias, akv_s[0][:, None, :], bkv_s[0][:, None, :], lfn16,
                            feat_major(cache_a_kv), feat_major(cache_b_kv), feat_major(cache_b_logf))
    heads_flat = lambda t: t[:, :N_HEADS, :HEAD_DIM].reshape(db, N_HEADS * HEAD_DIM)
    o_sample = jnp.concatenate([heads_flat(oa_s), heads_flat(ob_s)], axis=1).astype(MXU_DTYPE)

    cast = lambda w: w[0].astype(MXU_DTYPE)
    ffn = functools.partial(_ffn_call, ln_e=(ln_emb_g, ln_emb_b), w_o=cast(w_o), ln1=(ln1_g[0], ln1_b[0]),
                            w_gate=cast(w_gate), w_up=cast(w_up), w_down=cast(w_down), ln2=(ln2_g[0], ln2_b[0]),
                            alpha=alpha)
    y_prompt = ffn(x_prompt.reshape(bsz * seq, d), o_prompt.reshape(bsz * seq, -1),
                   tm=_row_tile(bsz * seq, 512)).reshape(bsz, seq, d)
    y_sample = ffn(x_sample.reshape(db, d), o_sample, tm=_row_tile(db, 512)).reshape(db, ds, d)

    def with_meta(m_part, x_part, tail):
        m_b = jnp.broadcast_to(m_part, (bsz,) + m_part.shape[1:])
        return jnp.concatenate([m_b, x_part], axis=1).reshape((1, bsz, n_meta + seq) + tail)

    return (y_prompt, y_sample,
            with_meta(akv_m, akv, (2, 1, HEAD_DIM)),
            with_meta(idxk_m, idxk, (HEAD_DIM,)),
            with_meta(bkv_m, bkv, (2, B_KV_HEADS, HEAD_DIM)),
            with_meta(lf_m, lf, (N_HEADS,)),
            akv_s[0].reshape(1, db, ds, 2, 1, HEAD_DIM),
            idxk_s[0].reshape(1, db, ds, HEAD_DIM),
            bkv_s[0].reshape(1, db, ds, 2, B_KV_HEADS, HEAD_DIM),
            lf_s[0].reshape(1, db, ds, N_HEADS))
```

```python
import functools

import jax
import jax.numpy as jnp
from jax import lax
from jax.experimental import pallas as pl
from jax.experimental.pallas import tpu as pltpu

HEAD_DIM = 64
N_HEADS = 8
B_KV_HEADS = 2
ROPE_HALF = 8
ROPE_THETA = 500000.0
TOPK_MAX = 256
PAGE = 128
LN_EPS = 1e-5
Q_BLOCK = 128
KEY_TILE = 256
LANES = 128
SUBLANES = 8
RADIX_STEPS = 4
ONES_ROWS = 16
C_ONES_AT = 32
INT_MIN = -2 ** 31
NEG_INF = float("-inf")
VMEM_LIMIT = 56 * 1024 * 1024

F32 = jnp.float32
BF16 = jnp.bfloat16
MXU_DTYPE = BF16

_NT = (((1,), (1,)), ((), ()))


def _dot(a, b):
    return jnp.dot(a, b, preferred_element_type=F32)


def _dot_nt(a, b):
    return lax.dot_general(a, b, _NT, preferred_element_type=F32)


def _layer_norm(x, g, b):
    mu = jnp.mean(x, axis=-1, keepdims=True)
    xc = x - mu
    var = jnp.mean(xc * xc, axis=-1, keepdims=True)
    return xc * lax.rsqrt(var + LN_EPS) * g + b


def _split3(x):
    hi = x.astype(MXU_DTYPE)
    r1 = x - hi.astype(F32)
    mid = r1.astype(MXU_DTYPE)
    lo = (r1 - mid.astype(F32)).astype(MXU_DTYPE)
    return hi, mid, lo


_C_QA, _C_QI, _C_QB, _C_KVA, _C_KIW, _C_KVB, _C_END = 0, 512, 1024, 1536, 1664, 1792, 2048


def _rope(v, cos, sa, sb):
    return v * cos + pltpu.roll(v, LANES - ROPE_HALF, 1) * sa + pltpu.roll(v, ROPE_HALF, 1) * sb


def _proj_body(x_ref, g_ref, b_ref, w_ref, bf_ref, cos1_ref, sa1_ref, sb1_ref, cos2_ref, sa2_ref, sb2_ref,
               tri_ref, c0_ref,
               q_ref, wc_ref, kidx_ref, kva_ref, kvb_ref, akv_ref, idxk_ref, bkv_ref, lf_ref, kext_ref,
               carry_ref):
    j = pl.program_id(1)
    tm = x_ref.shape[1]
    h = _layer_norm(x_ref[0], g_ref[...], b_ref[...])
    proj = _dot(h.astype(MXU_DTYPE), w_ref[...])

    cos1, sa1, sb1 = cos1_ref[...], sa1_ref[...], sb1_ref[...]
    scale = HEAD_DIM ** -0.5
    for g in range(_C_QB // LANES):
        seg = proj[:, g * LANES:(g + 1) * LANES]
        q_ref[0, :, g * LANES:(g + 1) * LANES] = (_rope(seg, cos1, sa1, sb1) * scale).astype(q_ref.dtype)
    q_ref[0, :, _C_QB:_C_KVA] = (proj[:, _C_QB:_C_KVA] * scale).astype(q_ref.dtype)

    cos2, sa2, sb2 = cos2_ref[...], sa2_ref[...], sb2_ref[...]
    kva = _rope(proj[:, _C_KVA:_C_KIW], cos2, sa2, sb2)
    akv_ref[0] = kva
    kva_ref[0] = kva.astype(kva_ref.dtype)
    kiw = _rope(proj[:, _C_KIW:_C_KVB], cos2, sa2, sb2)
    idxk_ref[0] = kiw[:, :HEAD_DIM]
    kidx_ref[0] = kiw[:, :HEAD_DIM].astype(kidx_ref.dtype)
    wi = kiw[:, HEAD_DIM:HEAD_DIM + N_HEADS] * (N_HEADS ** -0.5)
    z = kiw[:, HEAD_DIM + N_HEADS:HEAD_DIM + 2 * N_HEADS] + bf_ref[...]
    lf = jnp.minimum(z, 0.0) - jnp.log1p(jnp.exp(-jnp.abs(z)))
    lf_ref[0] = lf
    kvb = proj[:, _C_KVB:_C_END]
    bkv_ref[0] = kvb
    kvb_ref[0] = kvb.astype(kvb_ref.dtype)

    @pl.when(j == 0)
    def _():
        carry_ref[...] = c0_ref[...]

    tri = tri_ref[...]
    hi, mid, lo = _split3(lf)
    cs = _dot(tri, hi) + _dot(tri, mid) + _dot(tri, lo) + carry_ref[...]
    carry_ref[...] = cs[tm - 1:tm, :]
    wc_ref[0] = jnp.concatenate([wi, cs], axis=1)
    pieces = [-p.astype(F32) for p in _split3(cs)]
    lane = lax.broadcasted_iota(jnp.int32, (tm, LANES - C_ONES_AT), 1)
    ones_at = jnp.where(lane < 3, 1.0, 0.0)
    kext_ref[0] = jnp.concatenate(pieces + [jnp.zeros((tm, C_ONES_AT - 3 * N_HEADS), F32), ones_at],
                                  axis=1).astype(kext_ref.dtype)


def _rope_tables(pos):
    inv = ROPE_THETA ** (-jnp.arange(ROPE_HALF, dtype=F32) / ROPE_HALF)
    ang = pos.astype(F32)[:, None] * inv
    cos, sin = jnp.cos(ang), jnp.sin(ang)
    ones = jnp.ones((pos.shape[0], HEAD_DIM - 2 * ROPE_HALF), F32)
    zeros = jnp.zeros_like(ones)
    zh = jnp.zeros_like(sin)
    cos_h = jnp.concatenate([cos, cos, ones], 1)
    sa_h = jnp.concatenate([-sin, zh, zeros], 1)
    sb_h = jnp.concatenate([zh, sin, zeros], 1)
    one_h, zero_h = jnp.ones_like(cos_h), jnp.zeros_like(cos_h)
    t1 = [jnp.concatenate([t, t], 1) for t in (cos_h, sa_h, sb_h)]
    t2 = [jnp.concatenate([cos_h, one_h], 1), jnp.concatenate([sa_h, zero_h], 1),
          jnp.concatenate([sb_h, zero_h], 1)]
    return t1 + t2


def _proj_call(x3, pos, c0, w_cat, ln_g, ln_b, b_forget, tm):
    nb, s, d = x3.shape
    assert s % tm == 0
    tables = _rope_tables(pos)
    tri = jnp.tril(jnp.ones((tm, tm), F32)).astype(MXU_DTYPE)
    row = lambda b, j: (b, j, 0)
    const2 = lambda b, j: (0, 0)
    tab_spec = pl.BlockSpec((tm, LANES), lambda b, j: (j, 0))
    widths = (3 * 512, 2 * N_HEADS, HEAD_DIM, 2 * HEAD_DIM, 4 * HEAD_DIM, 2 * HEAD_DIM, HEAD_DIM, 4 * HEAD_DIM,
              N_HEADS, LANES)
    dtypes = (MXU_DTYPE, F32, MXU_DTYPE, MXU_DTYPE, MXU_DTYPE, F32, F32, F32, F32, MXU_DTYPE)
    return pl.pallas_call(
        _proj_body,
        grid=(nb, s // tm),
        in_specs=[pl.BlockSpec((1, tm, d), row),
                  pl.BlockSpec((1, d), const2), pl.BlockSpec((1, d), const2),
                  pl.BlockSpec(w_cat.shape, const2),
                  pl.BlockSpec((1, N_HEADS), const2)] + [tab_spec] * 6 +
                 [pl.BlockSpec((tm, tm), const2), pl.BlockSpec((1, N_HEADS), const2)],
        out_specs=[pl.BlockSpec((1, tm, w), row) for w in widths],
        out_shape=[jax.ShapeDtypeStruct((nb, s, w), dt) for w, dt in zip(widths, dtypes)],
        scratch_shapes=[pltpu.VMEM((1, N_HEADS), F32)],
        compiler_params=pltpu.CompilerParams(dimension_semantics=("arbitrary", "arbitrary"),
                                             vmem_limit_bytes=VMEM_LIMIT),
        name="proj",
    )(x3, ln_g.reshape(1, d), ln_b.reshape(1, d), w_cat, b_forget.reshape(1, N_HEADS), *tables, tri, c0)


F32_MAX = 3.4028234663852886e38


def _ordered_int_to_float(key):
    return pltpu.bitcast(key ^ ((key >> 31) & jnp.int32(0x7FFFFFFF)), F32)


def _select_topk(sc_ref, selb_ref, lt_ref, n_tiles, k):
    rows = sc_ref.shape[1]
    chains = 4
    part = chains * SUBLANES
    kf = float(k)

    def count(pred):
        def body(t, acc):
            hit = jnp.where(pred(sc_ref[t]), 1.0, 0.0)
            return acc + jnp.sum(hit.reshape(rows // part, part, LANES), axis=0)
        acc = lax.fori_loop(0, n_tiles, body, jnp.zeros((part, LANES), F32))
        return jnp.sum(acc, axis=0, keepdims=True)

    def n_unsettled(cnt):
        return jnp.sum(jnp.where(cnt != kf, 1.0, 0.0))

    n_adm = count(lambda x: x > NEG_INF)
    cur0 = jnp.full((1, LANES), INT_MIN, jnp.int32)
    cnt0 = jnp.where(n_adm <= kf, kf, n_adm)

    def cond(st):
        it, _, _, bad = st
        return jnp.logical_and(it < 32 // RADIX_STEPS, bad > 0.0)

    def body(st):
        it, cur, cnt, _ = st
        for step in range(RADIX_STEPS):
            bit = 31 - (it * RADIX_STEPS + step)
            cand = cur + jnp.left_shift(jnp.int32(1), bit)
            thr = _ordered_int_to_float(cand)
            c = count(lambda x: x >= thr)
            take = c >= kf
            cur = jnp.where(take, cand, cur)
            cnt = jnp.where(take, c, cnt)
        return it + 1, cur, cnt, n_unsettled(cnt)

    _, cur, cnt, _ = lax.while_loop(cond, body, (jnp.int32(0), cur0, cnt0, n_unsettled(cnt0)))
    has_ties = jnp.sum(jnp.where(cnt > kf, 1.0, 0.0)) > 0.0
    thr = jnp.where(cur == INT_MIN, -F32_MAX, _ordered_int_to_float(cur))

    @pl.when(jnp.logical_not(has_ties))
    def _():
        def body(t, carry):
            selb_ref[t] = jnp.where(sc_ref[t] >= thr, 0.0, NEG_INF)
            return carry
        lax.fori_loop(0, n_tiles, body, 0)

    @pl.when(has_ties)
    def _():
        need = kf - count(lambda x: x > thr)
        lt = lt_ref[...]

        def body(t, seen):
            st = sc_ref[t]
            eqf = jnp.where(st == thr, 1.0, 0.0)
            rank = _dot(lt, eqf.astype(MXU_DTYPE)) + seen
            keep_tie = jnp.where(rank <= need, eqf, 0.0)
            selb_ref[t] = jnp.where(st > thr, 0.0, jnp.where(keep_tie > 0.0, 0.0, NEG_INF))
            return seen + jnp.sum(eqf, axis=0, keepdims=True)
        lax.fori_loop(0, n_tiles, body, jnp.zeros((1, LANES), F32))


def _prompt_attn_body(qt_ref, wct_ref, kidx_ref, kva_ref, kvb_ref, vat_ref, vbt_ref, kext_ref,
                      kidxm_ref, kvam_ref, kvbm_ref, vatm_ref, vbtm_ref, kextm_ref, lt_ref, sel_ref,
                      o_ref,
                      kidx_c, kva_c, kb_c, vat_c, vbt_c, keys_ref, cmask_ref, selb_ref,
                      lga_ref, lgb_ref, acca_ref, accb_ref, *, n_meta, topk):
    j = pl.program_id(1)
    s = kidx_ref.shape[1]
    kp = kidx_c.shape[0]
    n_u = keys_ref.shape[0]
    cols = N_HEADS * Q_BLOCK
    groups = KEY_TILE // SUBLANES

    @pl.when(j == 0)
    def _():
        tail = kp - LANES - s
        for dst, lane0, m_src, x_src, width in ((kidx_c, 0, kidxm_ref, kidx_ref, HEAD_DIM),
                                                (kva_c, 0, kvam_ref, kva_ref, 2 * HEAD_DIM),
                                                (kb_c, 0, kvbm_ref, kvb_ref, 2 * HEAD_DIM),
                                                (kb_c, 2 * HEAD_DIM, kextm_ref, kext_ref, LANES)):
            lanes = slice(lane0, lane0 + width)
            dst[0:LANES, lanes] = m_src[:, 0:width]
            dst[LANES:LANES + s, lanes] = x_src[0, :, 0:width]
            if tail:
                dst[LANES + s:, lanes] = jnp.zeros((tail, width), dst.dtype)
        ones = jnp.ones((ONES_ROWS, kp), vat_c.dtype)
        for dst, m_src, x_src in ((vat_c, vatm_ref, vat_ref), (vbt_c, vbtm_ref, vbt_ref)):
            full = jnp.concatenate([m_src[...], x_src[0], jnp.zeros((m_src.shape[0], tail), dst.dtype)], axis=1)
            parts = []
            for g in range(m_src.shape[0] // HEAD_DIM):
                parts += [full[g * HEAD_DIM:(g + 1) * HEAD_DIM], ones]
            full = jnp.concatenate(parts, axis=0)
            for u in range(n_u):
                dst[u] = full[:, u * KEY_TILE:(u + 1) * KEY_TILE]

    qt = qt_ref[0]

    def heads_on_lanes(off):
        return jnp.concatenate([qt[off + h * HEAD_DIM:off + (h + 1) * HEAD_DIM, :] for h in range(N_HEADS)], axis=1)

    qa_t, qi_t, qb_t = heads_on_lanes(_C_QA), heads_on_lanes(_C_QI), heads_on_lanes(_C_QB)
    z_all = jnp.zeros((HEAD_DIM, cols), qt.dtype)
    z_half = jnp.zeros((HEAD_DIM, cols // 2), qt.dtype)
    qa_pad = jnp.concatenate([qa_t, z_all], axis=0)
    qb_bd = jnp.concatenate([jnp.concatenate([qb_t[:, :cols // 2], z_half], axis=1),
                             jnp.concatenate([z_half, qb_t[:, cols // 2:]], axis=1)], axis=0)
    wct = wct_ref[0]
    cq = jnp.concatenate([wct[N_HEADS + h:N_HEADS + h + 1, :] for h in range(N_HEADS)], axis=1)
    row = lax.broadcasted_iota(jnp.int32, (ONES_ROWS, cols), 0)
    cq_rows = jnp.zeros((ONES_ROWS, cols), F32)
    for i, piece in enumerate(_split3(cq)):
        cq_rows = jnp.where(row == i, piece.astype(F32), cq_rows)
    qb_ext = jnp.concatenate([qb_bd, sel_ref[...], cq_rows.astype(qt.dtype),
                              jnp.zeros((LANES - C_ONES_AT - ONES_ROWS, cols), qt.dtype)], axis=0)

    per = KEY_TILE // LANES
    n_tiles = jnp.right_shift(j + 1 + per, per.bit_length() - 1)

    def tile_rows(u):
        return pl.ds(pl.multiple_of(u * KEY_TILE, KEY_TILE), KEY_TILE)

    def pass1(q_t, k_tile, bias, lg_ref):
        def step(u, m):
            lg = _dot(k_tile(u), q_t)
            ms = []
            for h in range(N_HEADS):
                hs = slice(h * Q_BLOCK, (h + 1) * Q_BLOCK)
                lgh = lg[:, hs] + bias(u, h)
                lg_ref[u, :, hs] = lgh
                ms.append(jnp.max(lgh.reshape(groups, SUBLANES, Q_BLOCK), axis=0))
            return jnp.maximum(m, jnp.concatenate(ms, axis=1))
        return step

    def pass2(lg_ref, m_row, pv):
        def step(u):
            p = jnp.exp(lg_ref[u] - m_row)
            pv(u, p.astype(MXU_DTYPE))
        return step

    def col_max(m):
        return jnp.max(m, axis=0, keepdims=True)

    m_init = jnp.full((SUBLANES, cols), NEG_INF, F32)
    v_rows = HEAD_DIM + ONES_ROWS

    def pv_a(u, p):
        acca_ref[...] += _dot(vat_c[u], p)

    def pv_b(u, p):
        vt = vbt_c[u]
        accb_ref[:, 0:cols // 2] += _dot(vt[0:v_rows], p[:, 0:cols // 2])
        accb_ref[:, cols // 2:] += _dot(vt[v_rows:], p[:, cols // 2:])

    def normalised(acc_ref):
        return acc_ref[0:HEAD_DIM, :] / acc_ref[HEAD_DIM:HEAD_DIM + 1, :]

    a_pass1 = pass1(qa_pad, lambda u: kva_c[tile_rows(u), :], lambda u, h: selb_ref[u], lga_ref)
    b_pass1 = pass1(qb_ext, lambda u: kb_c[tile_rows(u), :], lambda u, h: cmask_ref[u], lgb_ref)

    def idx_step(u):
        d = _dot(kidx_c[tile_rows(u), :], qi_t)
        sc = jnp.zeros((KEY_TILE, Q_BLOCK), F32)
        for h in range(N_HEADS):
            sc = sc + wct[h:h + 1, :] * jnp.maximum(d[:, h * Q_BLOCK:(h + 1) * Q_BLOCK], 0.0)
        key_i = u * KEY_TILE + lax.broadcasted_iota(jnp.int32, (KEY_TILE, Q_BLOCK), 0)
        qry_i = lax.broadcasted_iota(jnp.int32, (KEY_TILE, Q_BLOCK), 1)
        valid = jnp.logical_or(key_i < n_meta,
                               jnp.logical_and(key_i >= LANES, key_i - LANES <= j * Q_BLOCK + qry_i))
        keys_ref[u] = jnp.where(valid, sc, NEG_INF)
        cmask_ref[u] = jnp.where(valid, 0.0, NEG_INF)

    def over_tiles(step, carry):
        def pair(i, c):
            return step(2 * i + 1, step(2 * i, c))
        carry = lax.fori_loop(0, jnp.right_shift(n_tiles, 1), pair, carry)
        return lax.cond(jnp.bitwise_and(n_tiles, 1) == 1, lambda c: step(n_tiles - 1, c), lambda c: c, carry)

    def idx_and_b1(u, m_b):
        idx_step(u)
        return b_pass1(u, m_b)
    m_b = col_max(over_tiles(idx_and_b1, m_init))

    _select_topk(keys_ref, selb_ref, lt_ref, n_tiles, topk)

    acca_ref[...] = jnp.zeros(acca_ref.shape, F32)
    accb_ref[...] = jnp.zeros(accb_ref.shape, F32)
    b_pass2 = pass2(lgb_ref, m_b, pv_b)

    def a1_and_b2(u, m_a):
        b_pass2(u)
        return a_pass1(u, m_a)
    m_a = col_max(over_tiles(a1_and_b2, m_init))
    a_pass2 = pass2(lga_ref, m_a, pv_a)

    def a2(u, carry):
        a_pass2(u)
        return carry
    over_tiles(a2, 0)

    out_a = normalised(acca_ref)
    for h in range(N_HEADS):
        o_ref[0, h * HEAD_DIM:(h + 1) * HEAD_DIM, :] = out_a[:, h * Q_BLOCK:(h + 1) * Q_BLOCK].astype(o_ref.dtype)
    out_b = normalised(accb_ref)
    for h in range(N_HEADS):
        o_ref[0, (N_HEADS + h) * HEAD_DIM:(N_HEADS + h + 1) * HEAD_DIM, :] = (
            out_b[:, h * Q_BLOCK:(h + 1) * Q_BLOCK].astype(o_ref.dtype))


def _prompt_attn(qt, wct, kidx, kva, kvb, vat, vbt, kext, kidx_m, kva_m, kvb_m, vat_m, vbt_m, kext_m, n_meta,
                 topk):
    nb, _, s = qt.shape
    assert s % Q_BLOCK == 0
    nq = s // Q_BLOCK
    per = KEY_TILE // LANES
    n_u = (nq + per) // per
    kp = n_u * KEY_TILE
    cols = N_HEADS * Q_BLOCK
    v_rows = HEAD_DIM + ONES_ROWS
    lt = jnp.tril(jnp.ones((KEY_TILE, KEY_TILE), F32)).astype(MXU_DTYPE)
    sel_row, sel_col = jnp.arange(C_ONES_AT)[:, None], jnp.arange(cols)[None, :]
    sel = jnp.where((sel_row < 3 * N_HEADS) & (sel_row % N_HEADS == sel_col // Q_BLOCK), 1.0, 0.0).astype(MXU_DTYPE)
    qblk = lambda b, j: (b, 0, j)
    per_b = lambda b, j: (b, 0, 0)
    const2 = lambda b, j: (0, 0)
    whole = lambda a: pl.BlockSpec(a.shape, const2)
    body = functools.partial(_prompt_attn_body, n_meta=n_meta, topk=topk)
    return pl.pallas_call(
        body,
        grid=(nb, nq),
        in_specs=[pl.BlockSpec((1, qt.shape[1], Q_BLOCK), qblk),
                  pl.BlockSpec((1, wct.shape[1], Q_BLOCK), qblk)] +
                 [pl.BlockSpec((1,) + a.shape[1:], per_b) for a in (kidx, kva, kvb, vat, vbt, kext)] +
                 [whole(a) for a in (kidx_m, kva_m, kvb_m, vat_m, vbt_m, kext_m, lt, sel)],
        out_specs=pl.BlockSpec((1, 2 * N_HEADS * HEAD_DIM, Q_BLOCK), qblk),
        out_shape=jax.ShapeDtypeStruct((nb, 2 * N_HEADS * HEAD_DIM, s), MXU_DTYPE),
        scratch_shapes=[pltpu.VMEM((kp, HEAD_DIM), MXU_DTYPE),
                        pltpu.VMEM((kp, 2 * HEAD_DIM), MXU_DTYPE),
                        pltpu.VMEM((kp, 2 * HEAD_DIM + LANES), MXU_DTYPE),
                        pltpu.VMEM((n_u, v_rows, KEY_TILE), MXU_DTYPE),
                        pltpu.VMEM((n_u, B_KV_HEADS * v_rows, KEY_TILE), MXU_DTYPE),
                        pltpu.VMEM((n_u, KEY_TILE, Q_BLOCK), F32),
                        pltpu.VMEM((n_u, KEY_TILE, Q_BLOCK), F32),
                        pltpu.VMEM((n_u, KEY_TILE, Q_BLOCK), F32),
                        pltpu.VMEM((n_u, KEY_TILE, cols), F32),
                        pltpu.VMEM((n_u, KEY_TILE, cols), F32),
                        pltpu.VMEM((v_rows, cols), F32),
                        pltpu.VMEM((v_rows, cols), F32)],
        compiler_params=pltpu.CompilerParams(dimension_semantics=("arbitrary", "arbitrary"),
                                             vmem_limit_bytes=VMEM_LIMIT),
        name="prompt_attn",
    )(qt, wct, kidx, kva, kvb, vat, vbt, kext, kidx_m, kva_m, kvb_m, vat_m, vbt_m, kext_m, lt, sel)


def _lane_window(buf, slot, p):
    return buf.at[slot, :, pl.ds(pl.multiple_of(p * PAGE, PAGE), PAGE)]


def _row_window(buf, slot, p):
    rows = N_HEADS
    return buf.at[slot, pl.ds(pl.multiple_of(p * rows, rows), rows), :]


def _page_copy(stream, page, slot, p):
    src_hbm, buf, window, sem = stream
    return pltpu.make_async_copy(src_hbm.at[page], window(buf, slot, p), sem.at[slot])


def _start_pages(pt_ref, sample, slot, n_pages, streams):
    def one(p, carry):
        page = pt_ref[sample * n_pages + p]
        for stream in streams:
            _page_copy(stream, page, slot, p).start()
        return carry
    lax.fori_loop(0, n_pages, one, 0)


def _wait_pages(slot, n_pages, streams):
    def one(p, carry):
        for stream in streams:
            _page_copy(stream, 0, slot, p).wait()
        return carry
    lax.fori_loop(0, n_pages, one, 0)


def _prefetch_schedule(pt_ref, n_pages, streams):
    b = pl.program_id(0)
    nb = pl.num_programs(0)
    slot = lax.rem(b, 2)

    @pl.when(b == 0)
    def _():
        _start_pages(pt_ref, 0, 0, n_pages, streams)

    @pl.when(b + 1 < nb)
    def _():
        _start_pages(pt_ref, b + 1, 1 - slot, n_pages, streams)

    _wait_pages(slot, n_pages, streams)
    return slot


SAMP_CHUNK = 1024


def _samp_scores_body(pt_ref, qi_ref, wi_ref, knew_ref, cache_hbm, out_ref, buf, sem, *, n_pages):
    slot = _prefetch_schedule(pt_ref, n_pages, [(cache_hbm, buf, _lane_window, sem)])
    past = n_pages * PAGE
    qi = qi_ref[0]
    w = wi_ref[0]
    for c in range(past // SAMP_CHUNK):
        cs = slice(c * SAMP_CHUNK, (c + 1) * SAMP_CHUNK)
        kt = buf[slot, :, cs].astype(MXU_DTYPE)
        d = _dot(qi, kt)
        out_ref[0, :, cs] = jnp.sum(w * jnp.maximum(d, 0.0), axis=0, keepdims=True)
    knew = knew_ref[0].astype(MXU_DTYPE).astype(F32)
    d_new = jnp.sum(qi.astype(F32) * knew, axis=1, keepdims=True)
    s_new = jnp.sum(w * jnp.maximum(d_new, 0.0), axis=0, keepdims=True)
    out_ref[0, :, past:past + LANES] = jnp.broadcast_to(s_new, (1, LANES))


def _samp_scores(page_table, qi16, wi16, k_new, cache_idx_k):
    db, n_pages = page_table.shape
    past = n_pages * PAGE
    assert past % SAMP_CHUNK == 0
    body = functools.partial(_samp_scores_body, n_pages=n_pages)
    row = lambda b, pt: (b, 0, 0)
    return pl.pallas_call(
        body,
        grid_spec=pltpu.PrefetchScalarGridSpec(
            num_scalar_prefetch=1, grid=(db,),
            in_specs=[pl.BlockSpec((1,) + qi16.shape[1:], row), pl.BlockSpec((1,) + wi16.shape[1:], row),
                      pl.BlockSpec((1, 1, HEAD_DIM), row), pl.BlockSpec(memory_space=pl.ANY)],
            out_specs=pl.BlockSpec((1, 1, past + LANES), row),
            scratch_shapes=[pltpu.VMEM((2, HEAD_DIM, past), F32), pltpu.SemaphoreType.DMA((2,))]),
        out_shape=jax.ShapeDtypeStruct((db, 1, past + LANES), F32),
        compiler_params=pltpu.CompilerParams(dimension_semantics=("arbitrary",), vmem_limit_bytes=VMEM_LIMIT),
        name="samp_scores",
    )(page_table.reshape(-1), qi16, wi16, k_new, cache_idx_k)


def _samp_select_body(sc_ref, lt_ref, selb_ref, keys_ref, *, n_pages, topk):
    n_t = keys_ref.shape[0]
    row = lax.broadcasted_iota(jnp.int32, (PAGE, LANES), 0)
    for t in range(n_t):
        key = sc_ref[t]
        if t == n_pages:
            key = jnp.where(row == 0, key, NEG_INF)
        keys_ref[t] = key
    _select_topk(keys_ref, selb_ref, lt_ref, n_t, topk)


def _samp_select(scores_tt, n_pages, topk):
    n_t, _, db = scores_tt.shape
    assert db % LANES == 0 and n_t == n_pages + 1
    lt = jnp.tril(jnp.ones((PAGE, PAGE), F32)).astype(MXU_DTYPE)
    body = functools.partial(_samp_select_body, n_pages=n_pages, topk=topk)
    blk = pl.BlockSpec((n_t, PAGE, LANES), lambda r: (0, 0, r))
    return pl.pallas_call(
        body,
        grid=(db // LANES,),
        in_specs=[blk, pl.BlockSpec(lt.shape, lambda r: (0, 0))],
        out_specs=blk,
        out_shape=jax.ShapeDtypeStruct(scores_tt.shape, F32),
        scratch_shapes=[pltpu.VMEM((n_t, PAGE, LANES), F32)],
        compiler_params=pltpu.CompilerParams(dimension_semantics=("arbitrary",), vmem_limit_bytes=VMEM_LIMIT),
        name="samp_select",
    )(scores_tt, lt)


def _samp_attn_body(pt_ref, qa_ref, qb_ref, bias_ref, anew_ref, bnew_ref, lfnew_ref, sl_ref,
                    akv_hbm, bkv_hbm, lft_hbm, oa_ref, ob_ref,
                    abuf, bbuf, lbuf, sem_a, sem_b, sem_l, abf, bbf, *, n_pages):
    slot = _prefetch_schedule(pt_ref, n_pages, [(akv_hbm, abuf, _lane_window, sem_a),
                                                (bkv_hbm, bbuf, _lane_window, sem_b),
                                                (lft_hbm, lbuf, _row_window, sem_l)])
    past = n_pages * PAGE
    n_chunks = past // SAMP_CHUNK
    pages_per_chunk = SAMP_CHUNK // PAGE

    def attend(q16, kv_bf, n_k, new_row, bias_chunk, bias_new):
        k_rows, v_rows = slice(0, n_k), slice(n_k, 2 * n_k)
        lgs = []
        m = jnp.full((2 * N_HEADS, 1), NEG_INF, F32)
        for c in range(n_chunks):
            cs = slice(c * SAMP_CHUNK, (c + 1) * SAMP_CHUNK)
            lg = _dot(q16, kv_bf[k_rows, cs]) + bias_chunk(c)
            lgs.append(lg)
            m = jnp.maximum(m, jnp.max(lg, axis=1, keepdims=True))
        new_bf = new_row.astype(MXU_DTYPE).astype(F32)
        lg_new = jnp.sum(q16.astype(F32) * new_bf[:, k_rows], axis=1, keepdims=True) + bias_new
        m = jnp.maximum(m, lg_new)
        l = jnp.zeros((2 * N_HEADS, 1), F32)
        acc = jnp.zeros((2 * N_HEADS, n_k), F32)
        for c in range(n_chunks):
            cs = slice(c * SAMP_CHUNK, (c + 1) * SAMP_CHUNK)
            p = jnp.exp(lgs[c] - m)
            l = l + jnp.sum(p, axis=1, keepdims=True)
            acc = acc + _dot_nt(p.astype(MXU_DTYPE), kv_bf[v_rows, cs])
        p_new = jnp.exp(lg_new - m)
        l = l + p_new
        acc = acc + p_new.astype(MXU_DTYPE).astype(F32) * new_bf[:, v_rows]
        return acc / l

    for c in range(n_chunks):
        cs = slice(c * SAMP_CHUNK, (c + 1) * SAMP_CHUNK)
        abf[:, cs] = abuf[slot, :, cs].astype(abf.dtype)
    oa_ref[0] = attend(qa_ref[0], abf, HEAD_DIM, anew_ref[0],
                       lambda c: bias_ref[0, :, c * SAMP_CHUNK:(c + 1) * SAMP_CHUNK],
                       bias_ref[0, :, past:past + 1])

    lf = lbuf[slot]
    sl = sl_ref[...]
    hi, mid, lo = _split3(lf)
    rev = _dot(hi, sl) + _dot(mid, sl) + _dot(lo, sl)
    tot = rev[:, 0:1] + lf[:, 0:1]
    run = jnp.zeros((N_HEADS, 1), F32)
    rp = [None] * n_pages
    for p in reversed(range(n_pages)):
        ps = slice(p * N_HEADS, (p + 1) * N_HEADS)
        rp[p] = rev[ps] + run
        run = run + tot[ps]
    nq = lfnew_ref[0]
    zpad = jnp.zeros((N_HEADS, SAMP_CHUNK), F32)

    def bias_b(c):
        rpc = jnp.concatenate(rp[c * pages_per_chunk:(c + 1) * pages_per_chunk], axis=1)
        return jnp.concatenate([rpc, zpad], axis=0) + nq

    for c in range(n_chunks):
        cs = slice(c * SAMP_CHUNK, (c + 1) * SAMP_CHUNK)
        bbf[:, cs] = bbuf[slot, :, cs].astype(bbf.dtype)
    out_b = attend(qb_ref[0], bbf, 2 * HEAD_DIM, bnew_ref[0], bias_b, jnp.zeros((2 * N_HEADS, 1), F32))
    head = lax.broadcasted_iota(jnp.int32, out_b.shape, 0)
    second_group = head >= N_HEADS // B_KV_HEADS
    ob_ref[0] = jnp.where(second_group, pltpu.roll(out_b, HEAD_DIM, 1), out_b)


def _samp_attn(page_table, qa16, qb16, bias, a_new, b_new, lf_new16, cache_a, cache_b, cache_lft):
    db, n_pages = page_table.shape
    past = n_pages * PAGE
    sl = jnp.tril(jnp.ones((LANES, LANES), F32), k=-1).astype(MXU_DTYPE)
    body = functools.partial(_samp_attn_body, n_pages=n_pages)
    row = lambda b, pt: (b, 0, 0)
    any_spec = pl.BlockSpec(memory_space=pl.ANY)
    return pl.pallas_call(
        body,
        grid_spec=pltpu.PrefetchScalarGridSpec(
            num_scalar_prefetch=1, grid=(db,),
            in_specs=[pl.BlockSpec((1,) + qa16.shape[1:], row), pl.BlockSpec((1,) + qb16.shape[1:], row),
                      pl.BlockSpec((1,) + bias.shape[1:], row), pl.BlockSpec((1,) + a_new.shape[1:], row),
                      pl.BlockSpec((1,) + b_new.shape[1:], row), pl.BlockSpec((1,) + lf_new16.shape[1:], row),
                      pl.BlockSpec(sl.shape, lambda b, pt: (0, 0)), any_spec, any_spec, any_spec],
            out_specs=[pl.BlockSpec((1, 2 * N_HEADS, HEAD_DIM), row),
                       pl.BlockSpec((1, 2 * N_HEADS, 2 * HEAD_DIM), row)],
            scratch_shapes=[pltpu.VMEM((2, 2 * HEAD_DIM, past), F32), pltpu.VMEM((2, 4 * HEAD_DIM, past), F32),
                            pltpu.VMEM((2, n_pages * N_HEADS, PAGE), F32),
                            pltpu.SemaphoreType.DMA((2,)), pltpu.SemaphoreType.DMA((2,)),
                            pltpu.SemaphoreType.DMA((2,)),
                            pltpu.VMEM((2 * HEAD_DIM, past), MXU_DTYPE), pltpu.VMEM((4 * HEAD_DIM, past), MXU_DTYPE)]),
        out_shape=[jax.ShapeDtypeStruct((db, 2 * N_HEADS, HEAD_DIM), F32),
                   jax.ShapeDtypeStruct((db, 2 * N_HEADS, 2 * HEAD_DIM), F32)],
        compiler_params=pltpu.CompilerParams(dimension_semantics=("arbitrary",), vmem_limit_bytes=VMEM_LIMIT),
        name="samp_attn",
    )(page_table.reshape(-1), qa16, qb16, bias, a_new, b_new, lf_new16, sl, cache_a, cache_b, cache_lft)


def _ffn_body(x_ref, o_ref, ge_ref, be_ref, wo_ref, g1_ref, b1_ref, wg_ref, wu_ref, wd_ref, g2_ref, b2_ref,
              y_ref, *, alpha, ff_chunk):
    h = _layer_norm(x_ref[...], ge_ref[...], be_ref[...])
    mixed = _dot(o_ref[...], wo_ref[...])
    h1 = _layer_norm(alpha * h + mixed, g1_ref[...], b1_ref[...])
    h1b = h1.astype(MXU_DTYPE)
    d_ff = wg_ref.shape[1]
    ffn = jnp.zeros(h1.shape, F32)
    for c in range(d_ff // ff_chunk):
        cs = slice(c * ff_chunk, (c + 1) * ff_chunk)
        gate = _dot(h1b, wg_ref[:, cs])
        up = _dot(h1b, wu_ref[:, cs])
        act = (gate * jax.nn.sigmoid(gate)) * up
        ffn = ffn + _dot(act.astype(MXU_DTYPE), wd_ref[cs, :])
    y_ref[...] = _layer_norm(alpha * h1 + ffn, g2_ref[...], b2_ref[...])


def _ffn_chunk(d_ff):
    for n in (11, 8, 6, 4, 2, 1):
        if d_ff % (n * LANES) == 0:
            return n * LANES
    raise ValueError(f"d_ff={d_ff} is not a multiple of {LANES}")


def _ffn_call(x2, o2, ln_e, w_o, ln1, w_gate, w_up, w_down, ln2, alpha, tm):
    m, d = x2.shape
    assert m % tm == 0
    d_ff = w_gate.shape[1]
    row = lambda i: (i, 0)
    const = lambda i: (0, 0)
    vec = pl.BlockSpec((1, d), const)
    resident = lambda shape: pl.BlockSpec(shape, const, pipeline_mode=pl.Buffered(1))
    body = functools.partial(_ffn_body, alpha=alpha, ff_chunk=_ffn_chunk(d_ff))
    return pl.pallas_call(
        body,
        grid=(m // tm,),
        in_specs=[pl.BlockSpec((tm, d), row), pl.BlockSpec((tm, o2.shape[1]), row), vec, vec,
                  resident(w_o.shape), vec, vec, resident(w_gate.shape), resident(w_up.shape),
                  resident(w_down.shape), vec, vec],
        out_specs=pl.BlockSpec((tm, d), row),
        out_shape=jax.ShapeDtypeStruct((m, d), F32),
        compiler_params=pltpu.CompilerParams(dimension_semantics=("arbitrary",), vmem_limit_bytes=VMEM_LIMIT),
        name="ffn",
    )(x2, o2, ln_e[0].reshape(1, d), ln_e[1].reshape(1, d), w_o, ln1[0].reshape(1, d), ln1[1].reshape(1, d),
      w_gate, w_up, w_down, ln2[0].reshape(1, d), ln2[1].reshape(1, d))


def _row_tile(n, cap):
    t = min(n, cap)
    while n % t:
        t //= 2
    return t


def kernel(x_prompt, x_sample, cache_a_kv, cache_idx_k, cache_b_kv, cache_b_logf, page_table, meta,
           ln_emb_g, ln_emb_b, w_in, b_forget, w_o, ln1_g, ln1_b, w_gate, w_up, w_down, ln2_g, ln2_b):
    depth = w_in.shape[0]
    assert depth == 1, "single-layer trunk only"
    bsz, seq, d = x_prompt.shape
    db, ds, _ = x_sample.shape
    assert ds == 1, "one new token per decode sequence"
    n_meta = meta.shape[0]
    n_pages = page_table.shape[1]
    past = n_pages * PAGE
    alpha = (2.0 * depth) ** 0.25
    topk_p = min(TOPK_MAX, seq // 4)
    topk_s = min(TOPK_MAX, (past + ds) // 4)

    sizes = (512, 64, 64, 512, 64, 8, 512, 128, 128, 8)
    offs = [0]
    for sz in sizes:
        offs.append(offs[-1] + sz)
    w0 = w_in[0]
    part = lambda i: w0[:, offs[i]:offs[i + 1]]
    qa_w, ka_w, va_w, qi_w, ki_w, wi_w, qb_w, kb_w, vb_w, fl_w = [part(i) for i in range(10)]
    pad_w = jnp.zeros((d, _C_KVB - _C_KIW - HEAD_DIM - 2 * N_HEADS), w0.dtype)
    w_cat = jnp.concatenate([qa_w, qi_w, qb_w, ka_w, va_w, ki_w, wi_w, fl_w, pad_w, kb_w, vb_w],
                            axis=1).astype(MXU_DTYPE)
    zero_c = jnp.zeros((1, N_HEADS), F32)
    proj = functools.partial(_proj_call, w_cat=w_cat, ln_g=ln_emb_g, ln_b=ln_emb_b, b_forget=b_forget[0])

    (_, wc_m, kidx_m, kva_m, kvb_m, akv_m, idxk_m, bkv_m, lf_m, kext_m) = proj(
        meta[None].astype(x_prompt.dtype), jnp.arange(n_meta), zero_c, tm=n_meta)
    c_meta = wc_m[0, :, N_HEADS:]
    (qcat, wc, kidx, kva, kvb, akv, idxk, bkv, lf, kext) = proj(
        x_prompt, n_meta + jnp.arange(seq), c_meta[n_meta - 1:n_meta], tm=_row_tile(seq, 512))
    (qcat_s, wc_s, _, _, _, akv_s, idxk_s, bkv_s, lf_s, _) = proj(
        x_sample.reshape(1, db, d), jnp.full((db,), past, jnp.int32), zero_c, tm=_row_tile(db, 512))

    pad_rows = lambda t: jnp.pad(t[0], ((0, LANES - n_meta), (0, 0)))
    t_last = lambda t: jnp.swapaxes(t, -1, -2)
    vat, vbt = t_last(kva[:, :, HEAD_DIM:]), t_last(kvb[:, :, 2 * HEAD_DIM:])
    vat_m, vbt_m = t_last(pad_rows(kva_m)[:, HEAD_DIM:]), t_last(pad_rows(kvb_m)[:, 2 * HEAD_DIM:])
    o_t = _prompt_attn(t_last(qcat), t_last(wc), kidx, kva, kvb, vat, vbt, kext,
                       pad_rows(kidx_m), pad_rows(kva_m), pad_rows(kvb_m), vat_m, vbt_m, pad_rows(kext_m),
                       n_meta, topk_p)
    o_prompt = t_last(o_t)

    qs = qcat_s[0].astype(F32).reshape(db, 3, N_HEADS, HEAD_DIM)
    pad_heads = lambda t: jnp.pad(t, ((0, 0), (0, N_HEADS), (0, 0)))
    qa16 = pad_heads(qs[:, 0]).astype(MXU_DTYPE)
    qi16 = pad_heads(qs[:, 1]).astype(MXU_DTYPE)
    half = N_HEADS // B_KV_HEADS
    head_group = jnp.arange(N_HEADS)[:, None] // half
    lane_group = jnp.arange(2 * HEAD_DIM)[None, :] // HEAD_DIM
    qb16 = pad_heads(jnp.where(head_group == lane_group, jnp.concatenate([qs[:, 2], qs[:, 2]], axis=2),
                               0.0)).astype(MXU_DTYPE)
    wi16 = pad_heads(wc_s[0, :, :N_HEADS, None])
    lfn16 = pad_heads(lf_s[0][:, :, None])
    feat_major = lambda c: jnp.moveaxis(c[0], 1, -1).reshape(c.shape[1], -1, PAGE)
    scores = _samp_scores(page_table, qi16, wi16, idxk_s[0][:, None, :], feat_major(cache_idx_k))
    n_t = n_pages + 1
    scores_tt = jnp.transpose(scores.reshape(db, n_t, PAGE), (1, 2, 0))
    selb_tt = _samp_select(scores_tt, n_pages, topk_s)
    bias = jnp.transpose(selb_tt, (2, 0, 1)).reshape(db, 1, n_t * PAGE)
    oa_s, ob_s = _samp_attn(page_table, qa16, qb16, bias, akv_s[0][:, None, :], bkv_s[0][:, None, :], lfn16,
                            feat_major(cache_a_kv), feat_major(cache_b_kv), feat_major(cache_b_logf))
    heads_flat = lambda t: t[:, :N_HEADS, :HEAD_DIM].reshape(db, N_HEADS * HEAD_DIM)
    o_sample = jnp.concatenate([heads_flat(oa_s), heads_flat(ob_s)], axis=1).astype(MXU_DTYPE)

    cast = lambda w: w[0].astype(MXU_DTYPE)
    ffn = functools.partial(_ffn_call, ln_e=(ln_emb_g, ln_emb_b), w_o=cast(w_o), ln1=(ln1_g[0], ln1_b[0]),
                            w_gate=cast(w_gate), w_up=cast(w_up), w_down=cast(w_down), ln2=(ln2_g[0], ln2_b[0]),
                            alpha=alpha)
    y_prompt = ffn(x_prompt.reshape(bsz * seq, d), o_prompt.reshape(bsz * seq, -1),
                   tm=_row_tile(bsz * seq, 512)).reshape(bsz, seq, d)
    y_sample = ffn(x_sample.reshape(db, d), o_sample, tm=_row_tile(db, 512)).reshape(db, ds, d)

    def with_meta(m_part, x_part, tail):
        m_b = jnp.broadcast_to(m_part, (bsz,) + m_part.shape[1:])
        return jnp.concatenate([m_b, x_part], axis=1).reshape((1, bsz, n_meta + seq) + tail)

    return (y_prompt, y_sample,
            with_meta(akv_m, akv, (2, 1, HEAD_DIM)),
            with_meta(idxk_m, idxk, (HEAD_DIM,)),
            with_meta(bkv_m, bkv, (2, B_KV_HEADS, HEAD_DIM)),
            with_meta(lf_m, lf, (N_HEADS,)),
            akv_s[0].reshape(1, db, ds, 2, 1, HEAD_DIM),
            idxk_s[0].reshape(1, db, ds, HEAD_DIM),
            bkv_s[0].reshape(1, db, ds, 2, B_KV_HEADS, HEAD_DIM),
            lf_s[0].reshape(1, db, ds, N_HEADS))
```

```python
import functools

import jax
import jax.numpy as jnp
from jax import lax
from jax.experimental import pallas as pl
from jax.experimental.pallas import tpu as pltpu

HEAD_DIM = 64
N_HEADS = 8
B_KV_HEADS = 2
ROPE_HALF = 8
ROPE_THETA = 500000.0
TOPK_MAX = 256
PAGE = 128
LN_EPS = 1e-5
Q_BLOCK = 128
KEY_TILE = 256
LANES = 128
SUBLANES = 8
RADIX_STEPS = 4
FAST_BITS = 25
ONES_ROWS = 16
C_ONES_AT = 32
INT_MIN = -2 ** 31
NEG_INF = float("-inf")
VMEM_LIMIT = 56 * 1024 * 1024

F32 = jnp.float32
BF16 = jnp.bfloat16
MXU_DTYPE = BF16

_NT = (((1,), (1,)), ((), ()))


def _dot(a, b):
    return jnp.dot(a, b, preferred_element_type=F32)


def _dot_nt(a, b):
    return lax.dot_general(a, b, _NT, preferred_element_type=F32)


def _layer_norm(x, g, b):
    mu = jnp.mean(x, axis=-1, keepdims=True)
    xc = x - mu
    var = jnp.mean(xc * xc, axis=-1, keepdims=True)
    return xc * lax.rsqrt(var + LN_EPS) * g + b


def _split3(x):
    hi = x.astype(MXU_DTYPE)
    r1 = x - hi.astype(F32)
    mid = r1.astype(MXU_DTYPE)
    lo = (r1 - mid.astype(F32)).astype(MXU_DTYPE)
    return hi, mid, lo


_C_QA, _C_QI, _C_QB, _C_KVA, _C_KIW, _C_KVB, _C_END = 0, 512, 1024, 1536, 1664, 1792, 2048


def _rope(v, cos, sa, sb):
    return v * cos + pltpu.roll(v, LANES - ROPE_HALF, 1) * sa + pltpu.roll(v, ROPE_HALF, 1) * sb


def _proj_body(x_ref, g_ref, b_ref, w_ref, bf_ref, cos1_ref, sa1_ref, sb1_ref, cos2_ref, sa2_ref, sb2_ref,
               tri_ref, c0_ref,
               q_ref, wc_ref, kidx_ref, kva_ref, kvb_ref, akv_ref, idxk_ref, bkv_ref, lf_ref, kext_ref,
               carry_ref, *, q_token_minor):
    j = pl.program_id(1)
    tm = x_ref.shape[1]
    h = _layer_norm(x_ref[0], g_ref[...], b_ref[...])
    proj = _dot(h.astype(MXU_DTYPE), w_ref[...])

    cos1, sa1, sb1 = cos1_ref[...], sa1_ref[...], sb1_ref[...]
    scale = HEAD_DIM ** -0.5
    for g in range(_C_KVA // LANES):
        gs = slice(g * LANES, (g + 1) * LANES)
        seg = proj[:, gs]
        seg = (_rope(seg, cos1, sa1, sb1) if g < _C_QB // LANES else seg) * scale
        if q_token_minor:
            q_ref[0, gs, :] = seg.T.astype(q_ref.dtype)
        else:
            q_ref[0, :, gs] = seg.astype(q_ref.dtype)

    cos2, sa2, sb2 = cos2_ref[...], sa2_ref[...], sb2_ref[...]
    kva = _rope(proj[:, _C_KVA:_C_KIW], cos2, sa2, sb2)
    akv_ref[0] = kva
    kva_ref[0] = kva.astype(kva_ref.dtype)
    kiw = _rope(proj[:, _C_KIW:_C_KVB], cos2, sa2, sb2)
    idxk_ref[0] = kiw[:, :HEAD_DIM]
    kidx_ref[0] = kiw[:, :HEAD_DIM].astype(kidx_ref.dtype)
    wi = kiw[:, HEAD_DIM:HEAD_DIM + N_HEADS] * (N_HEADS ** -0.5)
    z = kiw[:, HEAD_DIM + N_HEADS:HEAD_DIM + 2 * N_HEADS] + bf_ref[...]
    lf = jnp.minimum(z, 0.0) - jnp.log1p(jnp.exp(-jnp.abs(z)))
    lf_ref[0] = lf
    kvb = proj[:, _C_KVB:_C_END]
    bkv_ref[0] = kvb
    kvb_ref[0] = kvb.astype(kvb_ref.dtype)

    @pl.when(j == 0)
    def _():
        carry_ref[...] = c0_ref[...]

    tri = tri_ref[...]
    hi, mid, lo = _split3(lf)
    cs = _dot(tri, hi) + _dot(tri, mid) + _dot(tri, lo) + carry_ref[...]
    carry_ref[...] = cs[tm - 1:tm, :]
    wc_ref[0] = jnp.concatenate([wi, cs], axis=1)
    pieces = [-p.astype(F32) for p in _split3(cs)]
    lane = lax.broadcasted_iota(jnp.int32, (tm, LANES - C_ONES_AT), 1)
    ones_at = jnp.where(lane < 3, 1.0, 0.0)
    kext_ref[0] = jnp.concatenate(pieces + [jnp.zeros((tm, C_ONES_AT - 3 * N_HEADS), F32), ones_at],
                                  axis=1).astype(kext_ref.dtype)


def _rope_tables(pos):
    inv = ROPE_THETA ** (-jnp.arange(ROPE_HALF, dtype=F32) / ROPE_HALF)
    ang = pos.astype(F32)[:, None] * inv
    cos, sin = jnp.cos(ang), jnp.sin(ang)
    ones = jnp.ones((pos.shape[0], HEAD_DIM - 2 * ROPE_HALF), F32)
    zeros = jnp.zeros_like(ones)
    zh = jnp.zeros_like(sin)
    cos_h = jnp.concatenate([cos, cos, ones], 1)
    sa_h = jnp.concatenate([-sin, zh, zeros], 1)
    sb_h = jnp.concatenate([zh, sin, zeros], 1)
    one_h, zero_h = jnp.ones_like(cos_h), jnp.zeros_like(cos_h)
    t1 = [jnp.concatenate([t, t], 1) for t in (cos_h, sa_h, sb_h)]
    t2 = [jnp.concatenate([cos_h, one_h], 1), jnp.concatenate([sa_h, zero_h], 1),
          jnp.concatenate([sb_h, zero_h], 1)]
    return t1 + t2


def _proj_call(x3, pos, c0, w_cat, ln_g, ln_b, b_forget, tm, q_token_minor=False):
    nb, s, d = x3.shape
    assert s % tm == 0
    tables = _rope_tables(pos)
    tri = jnp.tril(jnp.ones((tm, tm), F32)).astype(MXU_DTYPE)
    row = lambda b, j: (b, j, 0)
    const2 = lambda b, j: (0, 0)
    tab_spec = pl.BlockSpec((tm, LANES), lambda b, j: (j, 0))
    widths = (3 * 512, 2 * N_HEADS, HEAD_DIM, 2 * HEAD_DIM, 4 * HEAD_DIM, 2 * HEAD_DIM, HEAD_DIM, 4 * HEAD_DIM,
              N_HEADS, LANES)
    dtypes = (MXU_DTYPE, F32, MXU_DTYPE, MXU_DTYPE, MXU_DTYPE, F32, F32, F32, F32, MXU_DTYPE)
    out_specs = [pl.BlockSpec((1, tm, w), row) for w in widths]
    out_shape = [jax.ShapeDtypeStruct((nb, s, w), dt) for w, dt in zip(widths, dtypes)]
    if q_token_minor:
        out_specs[0] = pl.BlockSpec((1, widths[0], tm), lambda b, j: (b, 0, j))
        out_shape[0] = jax.ShapeDtypeStruct((nb, widths[0], s), dtypes[0])
    return pl.pallas_call(
        functools.partial(_proj_body, q_token_minor=q_token_minor),
        grid=(nb, s // tm),
        in_specs=[pl.BlockSpec((1, tm, d), row),
                  pl.BlockSpec((1, d), const2), pl.BlockSpec((1, d), const2),
                  pl.BlockSpec(w_cat.shape, const2),
                  pl.BlockSpec((1, N_HEADS), const2)] + [tab_spec] * 6 +
                 [pl.BlockSpec((tm, tm), const2), pl.BlockSpec((1, N_HEADS), const2)],
        out_specs=out_specs,
        out_shape=out_shape,
        scratch_shapes=[pltpu.VMEM((1, N_HEADS), F32)],
        compiler_params=pltpu.CompilerParams(dimension_semantics=("arbitrary", "arbitrary"),
                                             vmem_limit_bytes=VMEM_LIMIT),
        name="proj",
    )(x3, ln_g.reshape(1, d), ln_b.reshape(1, d), w_cat, b_forget.reshape(1, N_HEADS), *tables, tri, c0)


F32_MAX = 3.4028234663852886e38


def _float_to_ordered_int(x):
    bits = pltpu.bitcast(x, jnp.int32)
    return bits ^ ((bits >> 31) & jnp.int32(0x7FFFFFFF))


def _ordered_int_to_float(key):
    return pltpu.bitcast(key ^ ((key >> 31) & jnp.int32(0x7FFFFFFF)), F32)


def _select_topk(sc_ref, selb_ref, lt_ref, n_tiles, k):
    rows = sc_ref.shape[1]
    chains = 4
    part = chains * SUBLANES
    kf = float(k)

    def count(pred):
        def body(t, acc):
            hit = jnp.where(pred(sc_ref[t]), 1.0, 0.0)
            return acc + jnp.sum(hit.reshape(rows // part, part, LANES), axis=0)
        acc = lax.fori_loop(0, n_tiles, body, jnp.zeros((part, LANES), F32))
        return jnp.sum(acc, axis=0, keepdims=True)

    def n_unsettled(cnt):
        return jnp.sum(jnp.where(cnt != kf, 1.0, 0.0))

    def col_max_all():
        def body(t, acc):
            return jnp.maximum(acc, jnp.max(sc_ref[t].reshape(rows // part, part, LANES), axis=0))
        acc = lax.fori_loop(0, n_tiles, body, jnp.full((part, LANES), NEG_INF, F32))
        return jnp.max(acc, axis=0, keepdims=True)

    n_adm = count(lambda x: x > NEG_INF)
    small = n_adm <= kf

    binade = 1 << 23
    below = (1 << FAST_BITS) - binade
    prefix = _float_to_ordered_int(col_max_all()) & jnp.int32(-binade)
    lo = prefix - below
    c_lo = count(lambda x: x >= _ordered_int_to_float(lo))
    ok = jnp.logical_or(small, jnp.logical_and(prefix >= INT_MIN + below, c_lo >= kf))
    fast = jnp.sum(jnp.where(ok, 0.0, 1.0)) == 0.0
    start_bit = jnp.where(fast, FAST_BITS - 1, 31)
    cur0 = jnp.where(fast, lo, INT_MIN)
    cnt0 = jnp.where(small, kf, jnp.where(fast, c_lo, n_adm))

    def cond(st):
        it, _, _, bad = st
        return jnp.logical_and(it * RADIX_STEPS <= start_bit, bad > 0.0)

    def body(st):
        it, cur, cnt, _ = st
        for step in range(RADIX_STEPS):
            bit = start_bit - (it * RADIX_STEPS + step)
            need_cnt = jnp.where(bit >= 0, kf, F32_MAX)
            cand = cur + jnp.left_shift(jnp.int32(1), jnp.maximum(bit, 0))
            thr = _ordered_int_to_float(cand)
            c = count(lambda x: x >= thr)
            take = c >= need_cnt
            cur = jnp.where(take, cand, cur)
            cnt = jnp.where(take, c, cnt)
        return it + 1, cur, cnt, n_unsettled(cnt)

    _, cur, cnt, _ = lax.while_loop(cond, body, (jnp.int32(0), cur0, cnt0, n_unsettled(cnt0)))
    has_ties = jnp.sum(jnp.where(cnt > kf, 1.0, 0.0)) > 0.0
    thr = jnp.where(small, -F32_MAX, _ordered_int_to_float(cur))

    @pl.when(jnp.logical_not(has_ties))
    def _():
        def body(t, carry):
            selb_ref[t] = jnp.where(sc_ref[t] >= thr, 0.0, NEG_INF)
            return carry
        lax.fori_loop(0, n_tiles, body, 0)

    @pl.when(has_ties)
    def _():
        need = kf - count(lambda x: x > thr)
        lt = lt_ref[...]

        def body(t, seen):
            st = sc_ref[t]
            eqf = jnp.where(st == thr, 1.0, 0.0)
            rank = _dot(lt, eqf.astype(MXU_DTYPE)) + seen
            keep_tie = jnp.where(rank <= need, eqf, 0.0)
            selb_ref[t] = jnp.where(st > thr, 0.0, jnp.where(keep_tie > 0.0, 0.0, NEG_INF))
            return seen + jnp.sum(eqf, axis=0, keepdims=True)
        lax.fori_loop(0, n_tiles, body, jnp.zeros((1, LANES), F32))


def _prompt_attn_body(qt_ref, wct_ref, kidx_ref, kva_ref, kvb_ref, vat_ref, vbt_ref, kext_ref,
                      kidxm_ref, kvam_ref, kvbm_ref, vatm_ref, vbtm_ref, kextm_ref, lt_ref, sel_ref,
                      o_ref,
                      kidx_c, kva_c, kb_c, vat_c, vbt_c, keys_ref, cmask_ref, selb_ref,
                      lga_ref, lgb_ref, acca_ref, accb_ref, *, n_meta, topk):
    j = pl.program_id(1)
    s = kidx_ref.shape[1]
    kp = kidx_c.shape[0]
    n_u = keys_ref.shape[0]
    cols = N_HEADS * Q_BLOCK
    groups = KEY_TILE // SUBLANES

    @pl.when(j == 0)
    def _():
        tail = kp - LANES - s
        for dst, lane0, m_src, x_src, width in ((kidx_c, 0, kidxm_ref, kidx_ref, HEAD_DIM),
                                                (kva_c, 0, kvam_ref, kva_ref, 2 * HEAD_DIM),
                                                (kb_c, 0, kvbm_ref, kvb_ref, 2 * HEAD_DIM),
                                                (kb_c, 2 * HEAD_DIM, kextm_ref, kext_ref, LANES)):
            lanes = slice(lane0, lane0 + width)
            dst[0:LANES, lanes] = m_src[:, 0:width]
            dst[LANES:LANES + s, lanes] = x_src[0, :, 0:width]
            if tail:
                dst[LANES + s:, lanes] = jnp.zeros((tail, width), dst.dtype)
        ones = jnp.ones((ONES_ROWS, kp), vat_c.dtype)
        for dst, m_src, x_src in ((vat_c, vatm_ref, vat_ref), (vbt_c, vbtm_ref, vbt_ref)):
            full = jnp.concatenate([m_src[...], x_src[0], jnp.zeros((m_src.shape[0], tail), dst.dtype)], axis=1)
            parts = []
            for g in range(m_src.shape[0] // HEAD_DIM):
                parts += [full[g * HEAD_DIM:(g + 1) * HEAD_DIM], ones]
            full = jnp.concatenate(parts, axis=0)
            for u in range(n_u):
                dst[u] = full[:, u * KEY_TILE:(u + 1) * KEY_TILE]

    qt = qt_ref[0]

    def heads_on_lanes(off):
        return jnp.concatenate([qt[off + h * HEAD_DIM:off + (h + 1) * HEAD_DIM, :] for h in range(N_HEADS)], axis=1)

    qa_t, qi_t, qb_t = heads_on_lanes(_C_QA), heads_on_lanes(_C_QI), heads_on_lanes(_C_QB)
    z_all = jnp.zeros((HEAD_DIM, cols), qt.dtype)
    z_half = jnp.zeros((HEAD_DIM, cols // 2), qt.dtype)
    qa_pad = jnp.concatenate([qa_t, z_all], axis=0)
    qb_bd = jnp.concatenate([jnp.concatenate([qb_t[:, :cols // 2], z_half], axis=1),
                             jnp.concatenate([z_half, qb_t[:, cols // 2:]], axis=1)], axis=0)
    wct = wct_ref[0]
    cq = jnp.concatenate([wct[N_HEADS + h:N_HEADS + h + 1, :] for h in range(N_HEADS)], axis=1)
    row = lax.broadcasted_iota(jnp.int32, (ONES_ROWS, cols), 0)
    cq_rows = jnp.zeros((ONES_ROWS, cols), F32)
    for i, piece in enumerate(_split3(cq)):
        cq_rows = jnp.where(row == i, piece.astype(F32), cq_rows)
    qb_ext = jnp.concatenate([qb_bd, sel_ref[...], cq_rows.astype(qt.dtype),
                              jnp.zeros((LANES - C_ONES_AT - ONES_ROWS, cols), qt.dtype)], axis=0)

    per = KEY_TILE // LANES
    n_tiles = jnp.right_shift(j + 1 + per, per.bit_length() - 1)

    def tile_rows(u):
        return pl.ds(pl.multiple_of(u * KEY_TILE, KEY_TILE), KEY_TILE)

    def pass1(q_t, k_tile, bias, lg_ref):
        def step(u, m):
            lg = _dot(k_tile(u), q_t)
            ms = []
            for h in range(N_HEADS):
                hs = slice(h * Q_BLOCK, (h + 1) * Q_BLOCK)
                lgh = lg[:, hs] + bias(u, h)
                lg_ref[u, :, hs] = lgh
                ms.append(jnp.max(lgh.reshape(groups, SUBLANES, Q_BLOCK), axis=0))
            return jnp.maximum(m, jnp.concatenate(ms, axis=1))
        return step

    def pass2(lg_ref, m_row, pv):
        def step(u):
            p = jnp.exp(lg_ref[u] - m_row)
            pv(u, p.astype(MXU_DTYPE))
        return step

    def col_max(m):
        return jnp.max(m, axis=0, keepdims=True)

    m_init = jnp.full((SUBLANES, cols), NEG_INF, F32)
    v_rows = HEAD_DIM + ONES_ROWS

    def pv_a(u, p):
        acca_ref[...] += _dot(vat_c[u], p)

    def pv_b(u, p):
        vt = vbt_c[u]
        accb_ref[:, 0:cols // 2] += _dot(vt[0:v_rows], p[:, 0:cols // 2])
        accb_ref[:, cols // 2:] += _dot(vt[v_rows:], p[:, cols // 2:])

    def normalised(acc_ref):
        return acc_ref[0:HEAD_DIM, :] / acc_ref[HEAD_DIM:HEAD_DIM + 1, :]

    a_pass1 = pass1(qa_pad, lambda u: kva_c[tile_rows(u), :], lambda u, h: selb_ref[u], lga_ref)
    b_pass1 = pass1(qb_ext, lambda u: kb_c[tile_rows(u), :], lambda u, h: cmask_ref[u], lgb_ref)

    def idx_step(u):
        d = _dot(kidx_c[tile_rows(u), :], qi_t)
        sc = jnp.zeros((KEY_TILE, Q_BLOCK), F32)
        for h in range(N_HEADS):
            sc = sc + wct[h:h + 1, :] * jnp.maximum(d[:, h * Q_BLOCK:(h + 1) * Q_BLOCK], 0.0)
        key_i = u * KEY_TILE + lax.broadcasted_iota(jnp.int32, (KEY_TILE, Q_BLOCK), 0)
        qry_i = lax.broadcasted_iota(jnp.int32, (KEY_TILE, Q_BLOCK), 1)
        valid = jnp.logical_or(key_i < n_meta,
                               jnp.logical_and(key_i >= LANES, key_i - LANES <= j * Q_BLOCK + qry_i))
        keys_ref[u] = jnp.where(valid, sc, NEG_INF)
        cmask_ref[u] = jnp.where(valid, 0.0, NEG_INF)

    def over_tiles(step, carry):
        def pair(i, c):
            return step(2 * i + 1, step(2 * i, c))
        carry = lax.fori_loop(0, jnp.right_shift(n_tiles, 1), pair, carry)
        return lax.cond(jnp.bitwise_and(n_tiles, 1) == 1, lambda c: step(n_tiles - 1, c), lambda c: c, carry)

    def idx_and_b1(u, m_b):
        idx_step(u)
        return b_pass1(u, m_b)
    m_b = col_max(over_tiles(idx_and_b1, m_init))

    _select_topk(keys_ref, selb_ref, lt_ref, n_tiles, topk)

    acca_ref[...] = jnp.zeros(acca_ref.shape, F32)
    accb_ref[...] = jnp.zeros(accb_ref.shape, F32)
    b_pass2 = pass2(lgb_ref, m_b, pv_b)

    def a1_and_b2(u, m_a):
        b_pass2(u)
        return a_pass1(u, m_a)
    m_a = col_max(over_tiles(a1_and_b2, m_init))
    a_pass2 = pass2(lga_ref, m_a, pv_a)

    def a2(u, carry):
        a_pass2(u)
        return carry
    over_tiles(a2, 0)

    out_a = normalised(acca_ref)
    for h in range(N_HEADS):
        o_ref[0, h * HEAD_DIM:(h + 1) * HEAD_DIM, :] = out_a[:, h * Q_BLOCK:(h + 1) * Q_BLOCK].astype(o_ref.dtype)
    out_b = normalised(accb_ref)
    for h in range(N_HEADS):
        o_ref[0, (N_HEADS + h) * HEAD_DIM:(N_HEADS + h + 1) * HEAD_DIM, :] = (
            out_b[:, h * Q_BLOCK:(h + 1) * Q_BLOCK].astype(o_ref.dtype))


def _prompt_attn(qt, wct, kidx, kva, kvb, vat, vbt, kext, kidx_m, kva_m, kvb_m, vat_m, vbt_m, kext_m, n_meta,
                 topk):
    nb, _, s = qt.shape
    assert s % Q_BLOCK == 0
    nq = s // Q_BLOCK
    per = KEY_TILE // LANES
    n_u = (nq + per) // per
    kp = n_u * KEY_TILE
    cols = N_HEADS * Q_BLOCK
    v_rows = HEAD_DIM + ONES_ROWS
    lt = jnp.tril(jnp.ones((KEY_TILE, KEY_TILE), F32)).astype(MXU_DTYPE)
    sel_row, sel_col = jnp.arange(C_ONES_AT)[:, None], jnp.arange(cols)[None, :]
    sel = jnp.where((sel_row < 3 * N_HEADS) & (sel_row % N_HEADS == sel_col // Q_BLOCK), 1.0, 0.0).astype(MXU_DTYPE)
    qblk = lambda b, j: (b, 0, j)
    per_b = lambda b, j: (b, 0, 0)
    const2 = lambda b, j: (0, 0)
    whole = lambda a: pl.BlockSpec(a.shape, const2)
    body = functools.partial(_prompt_attn_body, n_meta=n_meta, topk=topk)
    return pl.pallas_call(
        body,
        grid=(nb, nq),
        in_specs=[pl.BlockSpec((1, qt.shape[1], Q_BLOCK), qblk),
                  pl.BlockSpec((1, wct.shape[1], Q_BLOCK), qblk)] +
                 [pl.BlockSpec((1,) + a.shape[1:], per_b) for a in (kidx, kva, kvb, vat, vbt, kext)] +
                 [whole(a) for a in (kidx_m, kva_m, kvb_m, vat_m, vbt_m, kext_m, lt, sel)],
        out_specs=pl.BlockSpec((1, 2 * N_HEADS * HEAD_DIM, Q_BLOCK), qblk),
        out_shape=jax.ShapeDtypeStruct((nb, 2 * N_HEADS * HEAD_DIM, s), MXU_DTYPE),
        scratch_shapes=[pltpu.VMEM((kp, HEAD_DIM), MXU_DTYPE),
                        pltpu.VMEM((kp, 2 * HEAD_DIM), MXU_DTYPE),
                        pltpu.VMEM((kp, 2 * HEAD_DIM + LANES), MXU_DTYPE),
                        pltpu.VMEM((n_u, v_rows, KEY_TILE), MXU_DTYPE),
                        pltpu.VMEM((n_u, B_KV_HEADS * v_rows, KEY_TILE), MXU_DTYPE),
                        pltpu.VMEM((n_u, KEY_TILE, Q_BLOCK), F32),
                        pltpu.VMEM((n_u, KEY_TILE, Q_BLOCK), F32),
                        pltpu.VMEM((n_u, KEY_TILE, Q_BLOCK), F32),
                        pltpu.VMEM((n_u, KEY_TILE, cols), F32),
                        pltpu.VMEM((n_u, KEY_TILE, cols), F32),
                        pltpu.VMEM((v_rows, cols), F32),
                        pltpu.VMEM((v_rows, cols), F32)],
        compiler_params=pltpu.CompilerParams(dimension_semantics=("arbitrary", "arbitrary"),
                                             vmem_limit_bytes=VMEM_LIMIT),
        name="prompt_attn",
    )(qt, wct, kidx, kva, kvb, vat, vbt, kext, kidx_m, kva_m, kvb_m, vat_m, vbt_m, kext_m, lt, sel)


def _lane_window(buf, slot, p):
    return buf.at[slot, :, pl.ds(pl.multiple_of(p * PAGE, PAGE), PAGE)]


def _row_window(buf, slot, p):
    rows = N_HEADS
    return buf.at[slot, pl.ds(pl.multiple_of(p * rows, rows), rows), :]


def _page_copy(stream, page, slot, p):
    src_hbm, buf, window, sem = stream
    return pltpu.make_async_copy(src_hbm.at[page], window(buf, slot, p), sem.at[slot])


def _start_pages(pt_ref, sample, slot, n_pages, streams):
    def one(p, carry):
        page = pt_ref[sample * n_pages + p]
        for stream in streams:
            _page_copy(stream, page, slot, p).start()
        return carry
    lax.fori_loop(0, n_pages, one, 0, unroll=PAGE_UNROLL)


def _wait_pages(slot, n_pages, streams):
    def one(p, carry):
        for stream in streams:
            _page_copy(stream, 0, slot, p).wait()
        return carry
    lax.fori_loop(0, n_pages, one, 0, unroll=PAGE_UNROLL)


def _prefetch_schedule(pt_ref, n_pages, streams):
    b = pl.program_id(0)
    nb = pl.num_programs(0)
    slot = lax.rem(b, 2)

    @pl.when(b == 0)
    def _():
        _start_pages(pt_ref, 0, 0, n_pages, streams)

    @pl.when(b + 1 < nb)
    def _():
        _start_pages(pt_ref, b + 1, 1 - slot, n_pages, streams)

    _wait_pages(slot, n_pages, streams)
    return slot


SAMP_CHUNK = 1024
PAGE_UNROLL = 8


def _samp_scores_body(pt_ref, qi_ref, wi_ref, knew_ref, cache_hbm, out_ref, buf, sem, *, n_pages):
    slot = _prefetch_schedule(pt_ref, n_pages, [(cache_hbm, buf, _lane_window, sem)])
    past = n_pages * PAGE
    qi = qi_ref[0]
    w = wi_ref[0]
    for c in range(past // SAMP_CHUNK):
        cs = slice(c * SAMP_CHUNK, (c + 1) * SAMP_CHUNK)
        kt = buf[slot, :, cs].astype(MXU_DTYPE)
        d = _dot(qi, kt)
        out_ref[0, :, cs] = jnp.sum(w * jnp.maximum(d, 0.0), axis=0, keepdims=True)
    knew = knew_ref[0].astype(MXU_DTYPE).astype(F32)
    d_new = jnp.sum(qi.astype(F32) * knew, axis=1, keepdims=True)
    s_new = jnp.sum(w * jnp.maximum(d_new, 0.0), axis=0, keepdims=True)
    out_ref[0, :, past:past + LANES] = jnp.broadcast_to(s_new, (1, LANES))


def _samp_scores(page_table, qi16, wi16, k_new, cache_idx_k):
    db, n_pages = page_table.shape
    past = n_pages * PAGE
    assert past % SAMP_CHUNK == 0
    body = functools.partial(_samp_scores_body, n_pages=n_pages)
    row = lambda b, pt: (b, 0, 0)
    return pl.pallas_call(
        body,
        grid_spec=pltpu.PrefetchScalarGridSpec(
            num_scalar_prefetch=1, grid=(db,),
            in_specs=[pl.BlockSpec((1,) + qi16.shape[1:], row), pl.BlockSpec((1,) + wi16.shape[1:], row),
                      pl.BlockSpec((1, 1, HEAD_DIM), row), pl.BlockSpec(memory_space=pl.ANY)],
            out_specs=pl.BlockSpec((1, 1, past + LANES), row),
            scratch_shapes=[pltpu.VMEM((2, HEAD_DIM, past), F32), pltpu.SemaphoreType.DMA((2,))]),
        out_shape=jax.ShapeDtypeStruct((db, 1, past + LANES), F32),
        compiler_params=pltpu.CompilerParams(dimension_semantics=("arbitrary",), vmem_limit_bytes=VMEM_LIMIT),
        name="samp_scores",
    )(page_table.reshape(-1), qi16, wi16, k_new, cache_idx_k)


def _samp_select_body(sc_ref, lt_ref, selb_ref, keys_ref, *, n_pages, topk):
    n_t = keys_ref.shape[0]
    row = lax.broadcasted_iota(jnp.int32, (PAGE, LANES), 0)
    for t in range(n_t):
        key = sc_ref[t]
        if t == n_pages:
            key = jnp.where(row == 0, key, NEG_INF)
        keys_ref[t] = key
    _select_topk(keys_ref, selb_ref, lt_ref, n_t, topk)


def _samp_select(scores_tt, n_pages, topk):
    n_t, _, db = scores_tt.shape
    assert db % LANES == 0 and n_t == n_pages + 1
    lt = jnp.tril(jnp.ones((PAGE, PAGE), F32)).astype(MXU_DTYPE)
    body = functools.partial(_samp_select_body, n_pages=n_pages, topk=topk)
    blk = pl.BlockSpec((n_t, PAGE, LANES), lambda r: (0, 0, r))
    return pl.pallas_call(
        body,
        grid=(db // LANES,),
        in_specs=[blk, pl.BlockSpec(lt.shape, lambda r: (0, 0))],
        out_specs=blk,
        out_shape=jax.ShapeDtypeStruct(scores_tt.shape, F32),
        scratch_shapes=[pltpu.VMEM((n_t, PAGE, LANES), F32)],
        compiler_params=pltpu.CompilerParams(dimension_semantics=("arbitrary",), vmem_limit_bytes=VMEM_LIMIT),
        name="samp_select",
    )(scores_tt, lt)


def _samp_attn_body(pt_ref, qa_ref, qb_ref, bias_ref, anew_ref, bnew_ref, lfnew_ref, sl_ref,
                    akv_hbm, bkv_hbm, lft_hbm, oa_ref, ob_ref,
                    abuf, bbuf, lbuf, sem_a, sem_b, sem_l, abf, bbf, *, n_pages):
    slot = _prefetch_schedule(pt_ref, n_pages, [(akv_hbm, abuf, _lane_window, sem_a),
                                                (bkv_hbm, bbuf, _lane_window, sem_b),
                                                (lft_hbm, lbuf, _row_window, sem_l)])
    past = n_pages * PAGE
    n_chunks = past // SAMP_CHUNK
    pages_per_chunk = SAMP_CHUNK // PAGE

    def attend(q16, kv_bf, n_k, new_row, bias_chunk, bias_new):
        k_rows, v_rows = slice(0, n_k), slice(n_k, 2 * n_k)
        lgs = []
        m = jnp.full((2 * N_HEADS, 1), NEG_INF, F32)
        for c in range(n_chunks):
            cs = slice(c * SAMP_CHUNK, (c + 1) * SAMP_CHUNK)
            lg = _dot(q16, kv_bf[k_rows, cs]) + bias_chunk(c)
            lgs.append(lg)
            m = jnp.maximum(m, jnp.max(lg, axis=1, keepdims=True))
        new_bf = new_row.astype(MXU_DTYPE).astype(F32)
        lg_new = jnp.sum(q16.astype(F32) * new_bf[:, k_rows], axis=1, keepdims=True) + bias_new
        m = jnp.maximum(m, lg_new)
        l = jnp.zeros((2 * N_HEADS, 1), F32)
        acc = jnp.zeros((2 * N_HEADS, n_k), F32)
        for c in range(n_chunks):
            cs = slice(c * SAMP_CHUNK, (c + 1) * SAMP_CHUNK)
            p = jnp.exp(lgs[c] - m)
            l = l + jnp.sum(p, axis=1, keepdims=True)
            acc = acc + _dot_nt(p.astype(MXU_DTYPE), kv_bf[v_rows, cs])
        p_new = jnp.exp(lg_new - m)
        l = l + p_new
        acc = acc + p_new.astype(MXU_DTYPE).astype(F32) * new_bf[:, v_rows]
        return acc / l

    for c in range(n_chunks):
        cs = slice(c * SAMP_CHUNK, (c + 1) * SAMP_CHUNK)
        abf[:, cs] = abuf[slot, :, cs].astype(abf.dtype)
    oa_ref[0] = attend(qa_ref[0], abf, HEAD_DIM, anew_ref[0],
                       lambda c: bias_ref[0, :, c * SAMP_CHUNK:(c + 1) * SAMP_CHUNK],
                       bias_ref[0, :, past:past + 1])

    lf = lbuf[slot]
    sl = sl_ref[...]
    hi, mid, lo = _split3(lf)
    rev = _dot(hi, sl) + _dot(mid, sl) + _dot(lo, sl)
    tot = rev[:, 0:1] + lf[:, 0:1]
    run = jnp.zeros((N_HEADS, 1), F32)
    rp = [None] * n_pages
    for p in reversed(range(n_pages)):
        ps = slice(p * N_HEADS, (p + 1) * N_HEADS)
        rp[p] = rev[ps] + run
        run = run + tot[ps]
    nq = lfnew_ref[0]
    zpad = jnp.zeros((N_HEADS, SAMP_CHUNK), F32)

    def bias_b(c):
        rpc = jnp.concatenate(rp[c * pages_per_chunk:(c + 1) * pages_per_chunk], axis=1)
        return jnp.concatenate([rpc, zpad], axis=0) + nq

    for c in range(n_chunks):
        cs = slice(c * SAMP_CHUNK, (c + 1) * SAMP_CHUNK)
        bbf[:, cs] = bbuf[slot, :, cs].astype(bbf.dtype)
    out_b = attend(qb_ref[0], bbf, 2 * HEAD_DIM, bnew_ref[0], bias_b, jnp.zeros((2 * N_HEADS, 1), F32))
    head = lax.broadcasted_iota(jnp.int32, out_b.shape, 0)
    second_group = head >= N_HEADS // B_KV_HEADS
    ob_ref[0] = jnp.where(second_group, pltpu.roll(out_b, HEAD_DIM, 1), out_b)


def _samp_attn(page_table, qa16, qb16, bias, a_new, b_new, lf_new16, cache_a, cache_b, cache_lft):
    db, n_pages = page_table.shape
    past = n_pages * PAGE
    sl = jnp.tril(jnp.ones((LANES, LANES), F32), k=-1).astype(MXU_DTYPE)
    body = functools.partial(_samp_attn_body, n_pages=n_pages)
    row = lambda b, pt: (b, 0, 0)
    any_spec = pl.BlockSpec(memory_space=pl.ANY)
    return pl.pallas_call(
        body,
        grid_spec=pltpu.PrefetchScalarGridSpec(
            num_scalar_prefetch=1, grid=(db,),
            in_specs=[pl.BlockSpec((1,) + qa16.shape[1:], row), pl.BlockSpec((1,) + qb16.shape[1:], row),
                      pl.BlockSpec((1,) + bias.shape[1:], row), pl.BlockSpec((1,) + a_new.shape[1:], row),
                      pl.BlockSpec((1,) + b_new.shape[1:], row), pl.BlockSpec((1,) + lf_new16.shape[1:], row),
                      pl.BlockSpec(sl.shape, lambda b, pt: (0, 0)), any_spec, any_spec, any_spec],
            out_specs=[pl.BlockSpec((1, 2 * N_HEADS, HEAD_DIM), row),
                       pl.BlockSpec((1, 2 * N_HEADS, 2 * HEAD_DIM), row)],
            scratch_shapes=[pltpu.VMEM((2, 2 * HEAD_DIM, past), F32), pltpu.VMEM((2, 4 * HEAD_DIM, past), F32),
                            pltpu.VMEM((2, n_pages * N_HEADS, PAGE), F32),
                            pltpu.SemaphoreType.DMA((2,)), pltpu.SemaphoreType.DMA((2,)),
                            pltpu.SemaphoreType.DMA((2,)),
                            pltpu.VMEM((2 * HEAD_DIM, past), MXU_DTYPE), pltpu.VMEM((4 * HEAD_DIM, past), MXU_DTYPE)]),
        out_shape=[jax.ShapeDtypeStruct((db, 2 * N_HEADS, HEAD_DIM), F32),
                   jax.ShapeDtypeStruct((db, 2 * N_HEADS, 2 * HEAD_DIM), F32)],
        compiler_params=pltpu.CompilerParams(dimension_semantics=("arbitrary",), vmem_limit_bytes=VMEM_LIMIT),
        name="samp_attn",
    )(page_table.reshape(-1), qa16, qb16, bias, a_new, b_new, lf_new16, sl, cache_a, cache_b, cache_lft)


def _ffn_body(x_ref, o_ref, ge_ref, be_ref, wo_ref, g1_ref, b1_ref, wg_ref, wu_ref, wd_ref, g2_ref, b2_ref,
              y_ref, *, alpha, ff_chunk):
    h = _layer_norm(x_ref[...], ge_ref[...], be_ref[...])
    mixed = _dot(o_ref[...], wo_ref[...])
    h1 = _layer_norm(alpha * h + mixed, g1_ref[...], b1_ref[...])
    h1b = h1.astype(MXU_DTYPE)
    d_ff = wg_ref.shape[1]
    ffn = jnp.zeros(h1.shape, F32)
    for c in range(d_ff // ff_chunk):
        cs = slice(c * ff_chunk, (c + 1) * ff_chunk)
        gate = _dot(h1b, wg_ref[:, cs])
        up = _dot(h1b, wu_ref[:, cs])
        act = (gate * jax.nn.sigmoid(gate)) * up
        ffn = ffn + _dot(act.astype(MXU_DTYPE), wd_ref[cs, :])
    y_ref[...] = _layer_norm(alpha * h1 + ffn, g2_ref[...], b2_ref[...])


def _ffn_chunk(d_ff):
    for n in (11, 8, 6, 4, 2, 1):
        if d_ff % (n * LANES) == 0:
            return n * LANES
    raise ValueError(f"d_ff={d_ff} is not a multiple of {LANES}")


def _ffn_call(x2, o2, ln_e, w_o, ln1, w_gate, w_up, w_down, ln2, alpha, tm):
    m, d = x2.shape
    assert m % tm == 0
    d_ff = w_gate.shape[1]
    row = lambda i: (i, 0)
    const = lambda i: (0, 0)
    vec = pl.BlockSpec((1, d), const)
    resident = lambda shape: pl.BlockSpec(shape, const, pipeline_mode=pl.Buffered(1))
    body = functools.partial(_ffn_body, alpha=alpha, ff_chunk=_ffn_chunk(d_ff))
    return pl.pallas_call(
        body,
        grid=(m // tm,),
        in_specs=[pl.BlockSpec((tm, d), row), pl.BlockSpec((tm, o2.shape[1]), row), vec, vec,
                  resident(w_o.shape), vec, vec, resident(w_gate.shape), resident(w_up.shape),
                  resident(w_down.shape), vec, vec],
        out_specs=pl.BlockSpec((tm, d), row),
        out_shape=jax.ShapeDtypeStruct((m, d), F32),
        compiler_params=pltpu.CompilerParams(dimension_semantics=("arbitrary",), vmem_limit_bytes=VMEM_LIMIT),
        name="ffn",
    )(x2, o2, ln_e[0].reshape(1, d), ln_e[1].reshape(1, d), w_o, ln1[0].reshape(1, d), ln1[1].reshape(1, d),
      w_gate, w_up, w_down, ln2[0].reshape(1, d), ln2[1].reshape(1, d))


def _row_tile(n, cap):
    t = min(n, cap)
    while n % t:
        t //= 2
    return t


def kernel(x_prompt, x_sample, cache_a_kv, cache_idx_k, cache_b_kv, cache_b_logf, page_table, meta,
           ln_emb_g, ln_emb_b, w_in, b_forget, w_o, ln1_g, ln1_b, w_gate, w_up, w_down, ln2_g, ln2_b):
    depth = w_in.shape[0]
    assert depth == 1, "single-layer trunk only"
    bsz, seq, d = x_prompt.shape
    db, ds, _ = x_sample.shape
    assert ds == 1, "one new token per decode sequence"
    n_meta = meta.shape[0]
    n_pages = page_table.shape[1]
    past = n_pages * PAGE
    alpha = (2.0 * depth) ** 0.25
    topk_p = min(TOPK_MAX, seq // 4)
    topk_s = min(TOPK_MAX, (past + ds) // 4)

    sizes = (512, 64, 64, 512, 64, 8, 512, 128, 128, 8)
    offs = [0]
    for sz in sizes:
        offs.append(offs[-1] + sz)
    w0 = w_in[0]
    part = lambda i: w0[:, offs[i]:offs[i + 1]]
    qa_w, ka_w, va_w, qi_w, ki_w, wi_w, qb_w, kb_w, vb_w, fl_w = [part(i) for i in range(10)]
    pad_w = jnp.zeros((d, _C_KVB - _C_KIW - HEAD_DIM - 2 * N_HEADS), w0.dtype)
    w_cat = jnp.concatenate([qa_w, qi_w, qb_w, ka_w, va_w, ki_w, wi_w, fl_w, pad_w, kb_w, vb_w],
                            axis=1).astype(MXU_DTYPE)
    zero_c = jnp.zeros((1, N_HEADS), F32)
    proj = functools.partial(_proj_call, w_cat=w_cat, ln_g=ln_emb_g, ln_b=ln_emb_b, b_forget=b_forget[0])

    (_, wc_m, kidx_m, kva_m, kvb_m, akv_m, idxk_m, bkv_m, lf_m, kext_m) = proj(
        meta[None].astype(x_prompt.dtype), jnp.arange(n_meta), zero_c, tm=n_meta)
    c_meta = wc_m[0, :, N_HEADS:]
    (q_t, wc, kidx, kva, kvb, akv, idxk, bkv, lf, kext) = proj(
        x_prompt, n_meta + jnp.arange(seq), c_meta[n_meta - 1:n_meta], tm=_row_tile(seq, 512), q_token_minor=True)
    (qcat_s, wc_s, _, _, _, akv_s, idxk_s, bkv_s, lf_s, _) = proj(
        x_sample.reshape(1, db, d), jnp.full((db,), past, jnp.int32), zero_c, tm=_row_tile(db, 512))

    pad_rows = lambda t: jnp.pad(t[0], ((0, LANES - n_meta), (0, 0)))
    t_last = lambda t: jnp.swapaxes(t, -1, -2)
    vat, vbt = t_last(kva[:, :, HEAD_DIM:]), t_last(kvb[:, :, 2 * HEAD_DIM:])
    vat_m, vbt_m = t_last(pad_rows(kva_m)[:, HEAD_DIM:]), t_last(pad_rows(kvb_m)[:, 2 * HEAD_DIM:])
    o_t = _prompt_attn(q_t, t_last(wc), kidx, kva, kvb, vat, vbt, kext,
                       pad_rows(kidx_m), pad_rows(kva_m), pad_rows(kvb_m), vat_m, vbt_m, pad_rows(kext_m),
                       n_meta, topk_p)
    o_prompt = t_last(o_t)

    qs = qcat_s[0].astype(F32).reshape(db, 3, N_HEADS, HEAD_DIM)
    pad_heads = lambda t: jnp.pad(t, ((0, 0), (0, N_HEADS), (0, 0)))
    qa16 = pad_heads(qs[:, 0]).astype(MXU_DTYPE)
    qi16 = pad_heads(qs[:, 1]).astype(MXU_DTYPE)
    half = N_HEADS // B_KV_HEADS
    head_group = jnp.arange(N_HEADS)[:, None] // half
    lane_group = jnp.arange(2 * HEAD_DIM)[None, :] // HEAD_DIM
    qb16 = pad_heads(jnp.where(head_group == lane_group, jnp.concatenate([qs[:, 2], qs[:, 2]], axis=2),
                               0.0)).astype(MXU_DTYPE)
    wi16 = pad_heads(wc_s[0, :, :N_HEADS, None])
    lfn16 = pad_heads(lf_s[0][:, :, None])
    feat_major = lambda c: jnp.moveaxis(c[0], 1, -1).reshape(c.shape[1], -1, PAGE)
    scores = _samp_scores(page_table, qi16, wi16, idxk_s[0][:, None, :], feat_major(cache_idx_k))
    n_t = n_pages + 1
    scores_tt = jnp.transpose(scores.reshape(db, n_t, PAGE), (1, 2, 0))
    selb_tt = _samp_select(scores_tt, n_pages, topk_s)
    bias = jnp.transpose(selb_tt, (2, 0, 1)).reshape(db, 1, n_t * PAGE)
    oa_s, ob_s = _samp_attn(page_table, qa16, qb16, bias, akv_s[0][:, None, :], bkv_s[0][:, None, :], lfn16,
                            feat_major(cache_a_kv), feat_major(cache_b_kv), feat_major(cache_b_logf))
    heads_flat = lambda t: t[:, :N_HEADS, :HEAD_DIM].reshape(db, N_HEADS * HEAD_DIM)
    o_sample = jnp.concatenate([heads_flat(oa_s), heads_flat(ob_s)], axis=1).astype(MXU_DTYPE)

    cast = lambda w: w[0].astype(MXU_DTYPE)
    ffn = functools.partial(_ffn_call, ln_e=(ln_emb_g, ln_emb_b), w_o=cast(w_o), ln1=(ln1_g[0], ln1_b[0]),
                            w_gate=cast(w_gate), w_up=cast(w_up), w_down=cast(w_down), ln2=(ln2_g[0], ln2_b[0]),
                            alpha=alpha)
    y_prompt = ffn(x_prompt.reshape(bsz * seq, d), o_prompt.reshape(bsz * seq, -1),
                   tm=_row_tile(bsz * seq, 512)).reshape(bsz, seq, d)
    y_sample = ffn(x_sample.reshape(db, d), o_sample, tm=_row_tile(db, 512)).reshape(db, ds, d)

    def with_meta(m_part, x_part, tail):
        m_b = jnp.broadcast_to(m_part, (bsz,) + m_part.shape[1:])
        return jnp.concatenate([m_b, x_part], axis=1).reshape((1, bsz, n_meta + seq) + tail)

    return (y_prompt, y_sample,
            with_meta(akv_m, akv, (2, 1, HEAD_DIM)),
            with_meta(idxk_m, idxk, (HEAD_DIM,)),
            with_meta(bkv_m, bkv, (2, B_KV_HEADS, HEAD_DIM)),
            with_meta(lf_m, lf, (N_HEADS,)),
            akv_s[0].reshape(1, db, ds, 2, 1, HEAD_DIM),
            idxk_s[0].reshape(1, db, ds, HEAD_DIM),
            bkv_s[0].reshape(1, db, ds, 2, B_KV_HEADS, HEAD_DIM),
            lf_s[0].reshape(1, db, ds, N_HEADS))
```

```python
import functools

import jax
import jax.numpy as jnp
from jax import lax
from jax.experimental import pallas as pl
from jax.experimental.pallas import tpu as pltpu

HEAD_DIM = 64
N_HEADS = 8
B_KV_HEADS = 2
ROPE_HALF = 8
ROPE_THETA = 500000.0
TOPK_MAX = 256
PAGE = 128
LN_EPS = 1e-5
Q_BLOCK = 128
KEY_TILE = 256
LANES = 128
SUBLANES = 8
RADIX_STEPS = 4
ONES_ROWS = 16
C_ONES_AT = 32
INT_MIN = -2 ** 31
NEG_INF = float("-inf")
VMEM_LIMIT = 56 * 1024 * 1024

F32 = jnp.float32
BF16 = jnp.bfloat16
MXU_DTYPE = BF16

_NT = (((1,), (1,)), ((), ()))


def _dot(a, b):
    return jnp.dot(a, b, preferred_element_type=F32)


def _dot_nt(a, b):
    return lax.dot_general(a, b, _NT, preferred_element_type=F32)


def _layer_norm(x, g, b):
    mu = jnp.mean(x, axis=-1, keepdims=True)
    xc = x - mu
    var = jnp.mean(xc * xc, axis=-1, keepdims=True)
    return xc * lax.rsqrt(var + LN_EPS) * g + b


def _split3(x):
    hi = x.astype(MXU_DTYPE)
    r1 = x - hi.astype(F32)
    mid = r1.astype(MXU_DTYPE)
    lo = (r1 - mid.astype(F32)).astype(MXU_DTYPE)
    return hi, mid, lo


_C_QA, _C_QI, _C_QB, _C_KVA, _C_KIW, _C_KVB, _C_END = 0, 512, 1024, 1536, 1664, 1792, 2048


def _rope(v, cos, sa, sb):
    return v * cos + pltpu.roll(v, LANES - ROPE_HALF, 1) * sa + pltpu.roll(v, ROPE_HALF, 1) * sb


def _proj_body(x_ref, g_ref, b_ref, w_ref, bf_ref, cos1_ref, sa1_ref, sb1_ref, cos2_ref, sa2_ref, sb2_ref,
               tri_ref, c0_ref,
               q_ref, wc_ref, kidx_ref, kva_ref, kvb_ref, akv_ref, idxk_ref, bkv_ref, lf_ref, kext_ref,
               carry_ref, *, q_token_minor):
    j = pl.program_id(1)
    tm = x_ref.shape[1]
    h = _layer_norm(x_ref[0], g_ref[...], b_ref[...])
    proj = _dot(h.astype(MXU_DTYPE), w_ref[...])

    cos1, sa1, sb1 = cos1_ref[...], sa1_ref[...], sb1_ref[...]
    scale = HEAD_DIM ** -0.5
    for g in range(_C_KVA // LANES):
        gs = slice(g * LANES, (g + 1) * LANES)
        seg = proj[:, gs]
        seg = (_rope(seg, cos1, sa1, sb1) if g < _C_QB // LANES else seg) * scale
        if q_token_minor:
            q_ref[0, gs, :] = seg.T.astype(q_ref.dtype)
        else:
            q_ref[0, :, gs] = seg.astype(q_ref.dtype)

    cos2, sa2, sb2 = cos2_ref[...], sa2_ref[...], sb2_ref[...]
    kva = _rope(proj[:, _C_KVA:_C_KIW], cos2, sa2, sb2)
    akv_ref[0] = kva
    kva_ref[0] = kva.astype(kva_ref.dtype)
    kiw = _rope(proj[:, _C_KIW:_C_KVB], cos2, sa2, sb2)
    idxk_ref[0] = kiw[:, :HEAD_DIM]
    kidx_ref[0] = kiw[:, :HEAD_DIM].astype(kidx_ref.dtype)
    wi = kiw[:, HEAD_DIM:HEAD_DIM + N_HEADS] * (N_HEADS ** -0.5)
    z = kiw[:, HEAD_DIM + N_HEADS:HEAD_DIM + 2 * N_HEADS] + bf_ref[...]
    lf = jnp.minimum(z, 0.0) - jnp.log1p(jnp.exp(-jnp.abs(z)))
    lf_ref[0] = lf
    kvb = proj[:, _C_KVB:_C_END]
    bkv_ref[0] = kvb
    kvb_ref[0] = kvb.astype(kvb_ref.dtype)

    @pl.when(j == 0)
    def _():
        carry_ref[...] = c0_ref[...]

    tri = tri_ref[...]
    hi, mid, lo = _split3(lf)
    cs = _dot(tri, hi) + _dot(tri, mid) + _dot(tri, lo) + carry_ref[...]
    carry_ref[...] = cs[tm - 1:tm, :]
    wc_ref[0] = jnp.concatenate([wi, cs], axis=1)
    pieces = [-p.astype(F32) for p in _split3(cs)]
    lane = lax.broadcasted_iota(jnp.int32, (tm, LANES - C_ONES_AT), 1)
    ones_at = jnp.where(lane < 3, 1.0, 0.0)
    kext_ref[0] = jnp.concatenate(pieces + [jnp.zeros((tm, C_ONES_AT - 3 * N_HEADS), F32), ones_at],
                                  axis=1).astype(kext_ref.dtype)


def _rope_tables(pos):
    inv = ROPE_THETA ** (-jnp.arange(ROPE_HALF, dtype=F32) / ROPE_HALF)
    ang = pos.astype(F32)[:, None] * inv
    cos, sin = jnp.cos(ang), jnp.sin(ang)
    ones = jnp.ones((pos.shape[0], HEAD_DIM - 2 * ROPE_HALF), F32)
    zeros = jnp.zeros_like(ones)
    zh = jnp.zeros_like(sin)
    cos_h = jnp.concatenate([cos, cos, ones], 1)
    sa_h = jnp.concatenate([-sin, zh, zeros], 1)
    sb_h = jnp.concatenate([zh, sin, zeros], 1)
    one_h, zero_h = jnp.ones_like(cos_h), jnp.zeros_like(cos_h)
    t1 = [jnp.concatenate([t, t], 1) for t in (cos_h, sa_h, sb_h)]
    t2 = [jnp.concatenate([cos_h, one_h], 1), jnp.concatenate([sa_h, zero_h], 1),
          jnp.concatenate([sb_h, zero_h], 1)]
    return t1 + t2


def _proj_call(x3, pos, c0, w_cat, ln_g, ln_b, b_forget, tm, q_token_minor=False):
    nb, s, d = x3.shape
    assert s % tm == 0
    tables = _rope_tables(pos)
    tri = jnp.tril(jnp.ones((tm, tm), F32)).astype(MXU_DTYPE)
    row = lambda b, j: (b, j, 0)
    const2 = lambda b, j: (0, 0)
    tab_spec = pl.BlockSpec((tm, LANES), lambda b, j: (j, 0))
    widths = (3 * 512, 2 * N_HEADS, HEAD_DIM, 2 * HEAD_DIM, 4 * HEAD_DIM, 2 * HEAD_DIM, HEAD_DIM, 4 * HEAD_DIM,
              N_HEADS, LANES)
    dtypes = (MXU_DTYPE, F32, MXU_DTYPE, MXU_DTYPE, MXU_DTYPE, F32, F32, F32, F32, MXU_DTYPE)
    out_specs = [pl.BlockSpec((1, tm, w), row) for w in widths]
    out_shape = [jax.ShapeDtypeStruct((nb, s, w), dt) for w, dt in zip(widths, dtypes)]
    if q_token_minor:
        out_specs[0] = pl.BlockSpec((1, widths[0], tm), lambda b, j: (b, 0, j))
        out_shape[0] = jax.ShapeDtypeStruct((nb, widths[0], s), dtypes[0])
    return pl.pallas_call(
        functools.partial(_proj_body, q_token_minor=q_token_minor),
        grid=(nb, s // tm),
        in_specs=[pl.BlockSpec((1, tm, d), row),
                  pl.BlockSpec((1, d), const2), pl.BlockSpec((1, d), const2),
                  pl.BlockSpec(w_cat.shape, const2),
                  pl.BlockSpec((1, N_HEADS), const2)] + [tab_spec] * 6 +
                 [pl.BlockSpec((tm, tm), const2), pl.BlockSpec((1, N_HEADS), const2)],
        out_specs=out_specs,
        out_shape=out_shape,
        scratch_shapes=[pltpu.VMEM((1, N_HEADS), F32)],
        compiler_params=pltpu.CompilerParams(dimension_semantics=("arbitrary", "arbitrary"),
                                             vmem_limit_bytes=VMEM_LIMIT),
        name="proj",
    )(x3, ln_g.reshape(1, d), ln_b.reshape(1, d), w_cat, b_forget.reshape(1, N_HEADS), *tables, tri, c0)


F32_MAX = 3.4028234663852886e38


def _ordered_int_to_float(key):
    return pltpu.bitcast(key ^ ((key >> 31) & jnp.int32(0x7FFFFFFF)), F32)


def _select_topk(sc_ref, selb_ref, lt_ref, n_tiles, k):
    rows = sc_ref.shape[1]
    chains = 4
    part = chains * SUBLANES
    kf = float(k)

    def count(pred):
        def one(t, acc):
            hit = jnp.where(pred(sc_ref[t]), 1.0, 0.0)
            return acc + jnp.sum(hit.reshape(rows // part, part, LANES), axis=0)

        def pair(i, acc):
            return one(2 * i + 1, one(2 * i, acc))
        acc = lax.fori_loop(0, n_tiles // 2, pair, jnp.zeros((part, LANES), F32))
        acc = lax.cond(n_tiles % 2 == 1, lambda a: one(n_tiles - 1, a), lambda a: a, acc)
        return jnp.sum(acc, axis=0, keepdims=True)

    def n_unsettled(cnt):
        return jnp.sum(jnp.where(cnt != kf, 1.0, 0.0))

    n_adm = count(lambda x: x > NEG_INF)
    small = n_adm <= kf
    cur0 = jnp.full((1, LANES), INT_MIN, jnp.int32)
    cnt0 = jnp.where(small, kf, n_adm)

    def cond(st):
        it, _, _, bad = st
        return jnp.logical_and(it < 32 // RADIX_STEPS, bad > 0.0)

    def body(st):
        it, cur, cnt, _ = st
        for step in range(RADIX_STEPS):
            bit = 31 - (it * RADIX_STEPS + step)
            cand = cur + jnp.left_shift(jnp.int32(1), bit)
            thr = _ordered_int_to_float(cand)
            c = count(lambda x: x >= thr)
            take = c >= kf
            cur = jnp.where(take, cand, cur)
            cnt = jnp.where(take, c, cnt)
        return it + 1, cur, cnt, n_unsettled(cnt)

    _, cur, cnt, _ = lax.while_loop(cond, body, (jnp.int32(0), cur0, cnt0, n_unsettled(cnt0)))
    has_ties = jnp.sum(jnp.where(cnt > kf, 1.0, 0.0)) > 0.0
    thr = jnp.where(small, -F32_MAX, _ordered_int_to_float(cur))

    @pl.when(jnp.logical_not(has_ties))
    def _():
        def body(t, carry):
            selb_ref[t] = jnp.where(sc_ref[t] >= thr, 0.0, NEG_INF)
            return carry
        lax.fori_loop(0, n_tiles, body, 0)

    @pl.when(has_ties)
    def _():
        need = kf - count(lambda x: x > thr)
        lt = lt_ref[...]

        def body(t, seen):
            st = sc_ref[t]
            eqf = jnp.where(st == thr, 1.0, 0.0)
            rank = _dot(lt, eqf.astype(MXU_DTYPE)) + seen
            keep_tie = jnp.where(rank <= need, eqf, 0.0)
            selb_ref[t] = jnp.where(st > thr, 0.0, jnp.where(keep_tie > 0.0, 0.0, NEG_INF))
            return seen + jnp.sum(eqf, axis=0, keepdims=True)
        lax.fori_loop(0, n_tiles, body, jnp.zeros((1, LANES), F32))


def _prompt_attn_body(qt_ref, wct_ref, kidx_ref, kva_ref, kvb_ref, vat_ref, vbt_ref, kext_ref,
                      kidxm_ref, kvam_ref, kvbm_ref, vatm_ref, vbtm_ref, kextm_ref, lt_ref, sel_ref,
                      o_ref,
                      kidx_c, kva_c, kb_c, vat_c, vbt_c, keys_ref, cmask_ref, selb_ref,
                      lga_ref, lgb_ref, acca_ref, accb_ref, *, n_meta, topk):
    j = pl.program_id(1)
    s = kidx_ref.shape[1]
    kp = kidx_c.shape[0]
    n_u = keys_ref.shape[0]
    cols = N_HEADS * Q_BLOCK
    groups = KEY_TILE // SUBLANES

    @pl.when(j == 0)
    def _():
        tail = kp - LANES - s
        for dst, lane0, m_src, x_src, width in ((kidx_c, 0, kidxm_ref, kidx_ref, HEAD_DIM),
                                                (kva_c, 0, kvam_ref, kva_ref, 2 * HEAD_DIM),
                                                (kb_c, 0, kvbm_ref, kvb_ref, 2 * HEAD_DIM),
                                                (kb_c, 2 * HEAD_DIM, kextm_ref, kext_ref, LANES)):
            lanes = slice(lane0, lane0 + width)
            dst[0:LANES, lanes] = m_src[:, 0:width]
            dst[LANES:LANES + s, lanes] = x_src[0, :, 0:width]
            if tail:
                dst[LANES + s:, lanes] = jnp.zeros((tail, width), dst.dtype)
        ones = jnp.ones((ONES_ROWS, kp), vat_c.dtype)
        for dst, m_src, x_src in ((vat_c, vatm_ref, vat_ref), (vbt_c, vbtm_ref, vbt_ref)):
            full = jnp.concatenate([m_src[...], x_src[0], jnp.zeros((m_src.shape[0], tail), dst.dtype)], axis=1)
            parts = []
            for g in range(m_src.shape[0] // HEAD_DIM):
                parts += [full[g * HEAD_DIM:(g + 1) * HEAD_DIM], ones]
            full = jnp.concatenate(parts, axis=0)
            for u in range(n_u):
                dst[u] = full[:, u * KEY_TILE:(u + 1) * KEY_TILE]

    qt = qt_ref[0]

    def heads_on_lanes(off):
        return jnp.concatenate([qt[off + h * HEAD_DIM:off + (h + 1) * HEAD_DIM, :] for h in range(N_HEADS)], axis=1)

    qa_t, qi_t, qb_t = heads_on_lanes(_C_QA), heads_on_lanes(_C_QI), heads_on_lanes(_C_QB)
    z_all = jnp.zeros((HEAD_DIM, cols), qt.dtype)
    z_half = jnp.zeros((HEAD_DIM, cols // 2), qt.dtype)
    qa_pad = jnp.concatenate([qa_t, z_all], axis=0)
    qb_bd = jnp.concatenate([jnp.concatenate([qb_t[:, :cols // 2], z_half], axis=1),
                             jnp.concatenate([z_half, qb_t[:, cols // 2:]], axis=1)], axis=0)
    wct = wct_ref[0]
    cq = jnp.concatenate([wct[N_HEADS + h:N_HEADS + h + 1, :] for h in range(N_HEADS)], axis=1)
    row = lax.broadcasted_iota(jnp.int32, (ONES_ROWS, cols), 0)
    cq_rows = jnp.zeros((ONES_ROWS, cols), F32)
    for i, piece in enumerate(_split3(cq)):
        cq_rows = jnp.where(row == i, piece.astype(F32), cq_rows)
    qb_ext = jnp.concatenate([qb_bd, sel_ref[...], cq_rows.astype(qt.dtype),
                              jnp.zeros((LANES - C_ONES_AT - ONES_ROWS, cols), qt.dtype)], axis=0)

    per = KEY_TILE // LANES
    n_tiles = jnp.right_shift(j + 1 + per, per.bit_length() - 1)

    def tile_rows(u):
        return pl.ds(pl.multiple_of(u * KEY_TILE, KEY_TILE), KEY_TILE)

    def pass1(q_t, k_tile, bias, lg_ref):
        def step(u, m):
            lg = _dot(k_tile(u), q_t)
            ms = []
            for h in range(N_HEADS):
                hs = slice(h * Q_BLOCK, (h + 1) * Q_BLOCK)
                lgh = lg[:, hs] + bias(u, h)
                lg_ref[u, :, hs] = lgh
                ms.append(jnp.max(lgh.reshape(groups, SUBLANES, Q_BLOCK), axis=0))
            return jnp.maximum(m, jnp.concatenate(ms, axis=1))
        return step

    def pass2(lg_ref, m_row, pv):
        def step(u):
            p = jnp.exp(lg_ref[u] - m_row)
            pv(u, p.astype(MXU_DTYPE))
        return step

    def col_max(m):
        return jnp.max(m, axis=0, keepdims=True)

    m_init = jnp.full((SUBLANES, cols), NEG_INF, F32)
    v_rows = HEAD_DIM + ONES_ROWS

    def pv_a(u, p):
        acca_ref[...] += _dot(vat_c[u], p)

    def pv_b(u, p):
        vt = vbt_c[u]
        accb_ref[:, 0:cols // 2] += _dot(vt[0:v_rows], p[:, 0:cols // 2])
        accb_ref[:, cols // 2:] += _dot(vt[v_rows:], p[:, cols // 2:])

    def normalised(acc_ref):
        return acc_ref[0:HEAD_DIM, :] / acc_ref[HEAD_DIM:HEAD_DIM + 1, :]

    a_pass1 = pass1(qa_pad, lambda u: kva_c[tile_rows(u), :], lambda u, h: selb_ref[u], lga_ref)
    b_pass1 = pass1(qb_ext, lambda u: kb_c[tile_rows(u), :], lambda u, h: cmask_ref[u], lgb_ref)

    def idx_step(u):
        d = _dot(kidx_c[tile_rows(u), :], qi_t)
        sc = jnp.zeros((KEY_TILE, Q_BLOCK), F32)
        for h in range(N_HEADS):
            sc = sc + wct[h:h + 1, :] * jnp.maximum(d[:, h * Q_BLOCK:(h + 1) * Q_BLOCK], 0.0)
        key_i = u * KEY_TILE + lax.broadcasted_iota(jnp.int32, (KEY_TILE, Q_BLOCK), 0)
        qry_i = lax.broadcasted_iota(jnp.int32, (KEY_TILE, Q_BLOCK), 1)
        valid = jnp.logical_or(key_i < n_meta,
                               jnp.logical_and(key_i >= LANES, key_i - LANES <= j * Q_BLOCK + qry_i))
        keys_ref[u] = jnp.where(valid, sc, NEG_INF)
        cmask_ref[u] = jnp.where(valid, 0.0, NEG_INF)

    def over_tiles(step, carry):
        def pair(i, c):
            return step(2 * i + 1, step(2 * i, c))
        carry = lax.fori_loop(0, jnp.right_shift(n_tiles, 1), pair, carry)
        return lax.cond(jnp.bitwise_and(n_tiles, 1) == 1, lambda c: step(n_tiles - 1, c), lambda c: c, carry)

    def idx_and_b1(u, m_b):
        idx_step(u)
        return b_pass1(u, m_b)
    m_b = col_max(over_tiles(idx_and_b1, m_init))

    _select_topk(keys_ref, selb_ref, lt_ref, n_tiles, topk)

    acca_ref[...] = jnp.zeros(acca_ref.shape, F32)
    accb_ref[...] = jnp.zeros(accb_ref.shape, F32)
    b_pass2 = pass2(lgb_ref, m_b, pv_b)

    def a1_and_b2(u, m_a):
        b_pass2(u)
        return a_pass1(u, m_a)
    m_a = col_max(over_tiles(a1_and_b2, m_init))
    a_pass2 = pass2(lga_ref, m_a, pv_a)

    def a2(u, carry):
        a_pass2(u)
        return carry
    over_tiles(a2, 0)

    out_a = normalised(acca_ref)
    for h in range(N_HEADS):
        o_ref[0, h * HEAD_DIM:(h + 1) * HEAD_DIM, :] = out_a[:, h * Q_BLOCK:(h + 1) * Q_BLOCK].astype(o_ref.dtype)
    out_b = normalised(accb_ref)
    for h in range(N_HEADS):
        o_ref[0, (N_HEADS + h) * HEAD_DIM:(N_HEADS + h + 1) * HEAD_DIM, :] = (
            out_b[:, h * Q_BLOCK:(h + 1) * Q_BLOCK].astype(o_ref.dtype))


def _prompt_attn(qt, wct, kidx, kva, kvb, vat, vbt, kext, kidx_m, kva_m, kvb_m, vat_m, vbt_m, kext_m, n_meta,
                 topk):
    nb, _, s = qt.shape
    assert s % Q_BLOCK == 0
    nq = s // Q_BLOCK
    per = KEY_TILE // LANES
    n_u = (nq + per) // per
    kp = n_u * KEY_TILE
    cols = N_HEADS * Q_BLOCK
    v_rows = HEAD_DIM + ONES_ROWS
    lt = jnp.tril(jnp.ones((KEY_TILE, KEY_TILE), F32)).astype(MXU_DTYPE)
    sel_row, sel_col = jnp.arange(C_ONES_AT)[:, None], jnp.arange(cols)[None, :]
    sel = jnp.where((sel_row < 3 * N_HEADS) & (sel_row % N_HEADS == sel_col // Q_BLOCK), 1.0, 0.0).astype(MXU_DTYPE)
    qblk = lambda b, j: (b, 0, j)
    per_b = lambda b, j: (b, 0, 0)
    const2 = lambda b, j: (0, 0)
    whole = lambda a: pl.BlockSpec(a.shape, const2)
    body = functools.partial(_prompt_attn_body, n_meta=n_meta, topk=topk)
    return pl.pallas_call(
        body,
        grid=(nb, nq),
        in_specs=[pl.BlockSpec((1, qt.shape[1], Q_BLOCK), qblk),
                  pl.BlockSpec((1, wct.shape[1], Q_BLOCK), qblk)] +
                 [pl.BlockSpec((1,) + a.shape[1:], per_b) for a in (kidx, kva, kvb, vat, vbt, kext)] +
                 [whole(a) for a in (kidx_m, kva_m, kvb_m, vat_m, vbt_m, kext_m, lt, sel)],
        out_specs=pl.BlockSpec((1, 2 * N_HEADS * HEAD_DIM, Q_BLOCK), qblk),
        out_shape=jax.ShapeDtypeStruct((nb, 2 * N_HEADS * HEAD_DIM, s), MXU_DTYPE),
        scratch_shapes=[pltpu.VMEM((kp, HEAD_DIM), MXU_DTYPE),
                        pltpu.VMEM((kp, 2 * HEAD_DIM), MXU_DTYPE),
                        pltpu.VMEM((kp, 2 * HEAD_DIM + LANES), MXU_DTYPE),
                        pltpu.VMEM((n_u, v_rows, KEY_TILE), MXU_DTYPE),
                        pltpu.VMEM((n_u, B_KV_HEADS * v_rows, KEY_TILE), MXU_DTYPE),
                        pltpu.VMEM((n_u, KEY_TILE, Q_BLOCK), F32),
                        pltpu.VMEM((n_u, KEY_TILE, Q_BLOCK), F32),
                        pltpu.VMEM((n_u, KEY_TILE, Q_BLOCK), F32),
                        pltpu.VMEM((n_u, KEY_TILE, cols), F32),
                        pltpu.VMEM((n_u, KEY_TILE, cols), F32),
                        pltpu.VMEM((v_rows, cols), F32),
                        pltpu.VMEM((v_rows, cols), F32)],
        compiler_params=pltpu.CompilerParams(dimension_semantics=("arbitrary", "arbitrary"),
                                             vmem_limit_bytes=VMEM_LIMIT),
        name="prompt_attn",
    )(qt, wct, kidx, kva, kvb, vat, vbt, kext, kidx_m, kva_m, kvb_m, vat_m, vbt_m, kext_m, lt, sel)


def _lane_window(buf, slot, p):
    return buf.at[slot, :, pl.ds(pl.multiple_of(p * PAGE, PAGE), PAGE)]


def _row_window(buf, slot, p):
    rows = N_HEADS
    return buf.at[slot, pl.ds(pl.multiple_of(p * rows, rows), rows), :]


def _page_copy(stream, page, slot, p):
    src_hbm, buf, window, sem = stream
    return pltpu.make_async_copy(src_hbm.at[page], window(buf, slot, p), sem.at[slot])


def _start_pages(pt_ref, sample, slot, n_pages, streams):
    def one(p, carry):
        page = pt_ref[sample * n_pages + p]
        for stream in streams:
            _page_copy(stream, page, slot, p).start()
        return carry
    lax.fori_loop(0, n_pages, one, 0, unroll=PAGE_UNROLL)


def _wait_pages(slot, n_pages, streams):
    def one(p, carry):
        for stream in streams:
            _page_copy(stream, 0, slot, p).wait()
        return carry
    lax.fori_loop(0, n_pages, one, 0, unroll=PAGE_UNROLL)


def _prefetch_schedule(pt_ref, n_pages, streams):
    b = pl.program_id(0)
    nb = pl.num_programs(0)
    slot = lax.rem(b, 2)

    @pl.when(b == 0)
    def _():
        _start_pages(pt_ref, 0, 0, n_pages, streams)

    @pl.when(b + 1 < nb)
    def _():
        _start_pages(pt_ref, b + 1, 1 - slot, n_pages, streams)

    _wait_pages(slot, n_pages, streams)
    return slot


SAMP_CHUNK = 1024
PAGE_UNROLL = 8


def _samp_scores_body(pt_ref, qi_ref, wi_ref, knew_ref, cache_hbm, out_ref, buf, sem, *, n_pages):
    slot = _prefetch_schedule(pt_ref, n_pages, [(cache_hbm, buf, _lane_window, sem)])
    past = n_pages * PAGE
    qi = qi_ref[0]
    w = wi_ref[0]
    for c in range(past // SAMP_CHUNK):
        cs = slice(c * SAMP_CHUNK, (c + 1) * SAMP_CHUNK)
        kt = buf[slot, :, cs].astype(MXU_DTYPE)
        d = _dot(qi, kt)
        out_ref[0, :, cs] = jnp.sum(w * jnp.maximum(d, 0.0), axis=0, keepdims=True)
    knew = knew_ref[0].astype(MXU_DTYPE).astype(F32)
    d_new = jnp.sum(qi.astype(F32) * knew, axis=1, keepdims=True)
    s_new = jnp.sum(w * jnp.maximum(d_new, 0.0), axis=0, keepdims=True)
    out_ref[0, :, past:past + LANES] = jnp.broadcast_to(s_new, (1, LANES))


def _samp_scores(page_table, qi16, wi16, k_new, cache_idx_k):
    db, n_pages = page_table.shape
    past = n_pages * PAGE
    assert past % SAMP_CHUNK == 0
    body = functools.partial(_samp_scores_body, n_pages=n_pages)
    row = lambda b, pt: (b, 0, 0)
    return pl.pallas_call(
        body,
        grid_spec=pltpu.PrefetchScalarGridSpec(
            num_scalar_prefetch=1, grid=(db,),
            in_specs=[pl.BlockSpec((1,) + qi16.shape[1:], row), pl.BlockSpec((1,) + wi16.shape[1:], row),
                      pl.BlockSpec((1, 1, HEAD_DIM), row), pl.BlockSpec(memory_space=pl.ANY)],
            out_specs=pl.BlockSpec((1, 1, past + LANES), row),
            scratch_shapes=[pltpu.VMEM((2, HEAD_DIM, past), F32), pltpu.SemaphoreType.DMA((2,))]),
        out_shape=jax.ShapeDtypeStruct((db, 1, past + LANES), F32),
        compiler_params=pltpu.CompilerParams(dimension_semantics=("arbitrary",), vmem_limit_bytes=VMEM_LIMIT),
        name="samp_scores",
    )(page_table.reshape(-1), qi16, wi16, k_new, cache_idx_k)


def _samp_select_body(sc_ref, lt_ref, selb_ref, keys_ref, *, n_pages, topk):
    n_t = keys_ref.shape[0]
    row = lax.broadcasted_iota(jnp.int32, (PAGE, LANES), 0)
    for t in range(n_t):
        key = sc_ref[t]
        if t == n_pages:
            key = jnp.where(row == 0, key, NEG_INF)
        keys_ref[t] = key
    _select_topk(keys_ref, selb_ref, lt_ref, n_t, topk)


def _samp_select(scores_tt, n_pages, topk):
    n_t, _, db = scores_tt.shape
    assert db % LANES == 0 and n_t == n_pages + 1
    lt = jnp.tril(jnp.ones((PAGE, PAGE), F32)).astype(MXU_DTYPE)
    body = functools.partial(_samp_select_body, n_pages=n_pages, topk=topk)
    blk = pl.BlockSpec((n_t, PAGE, LANES), lambda r: (0, 0, r))
    return pl.pallas_call(
        body,
        grid=(db // LANES,),
        in_specs=[blk, pl.BlockSpec(lt.shape, lambda r: (0, 0))],
        out_specs=blk,
        out_shape=jax.ShapeDtypeStruct(scores_tt.shape, F32),
        scratch_shapes=[pltpu.VMEM((n_t, PAGE, LANES), F32)],
        compiler_params=pltpu.CompilerParams(dimension_semantics=("arbitrary",), vmem_limit_bytes=VMEM_LIMIT),
        name="samp_select",
    )(scores_tt, lt)


def _samp_attn_body(pt_ref, qa_ref, qb_ref, bias_ref, anew_ref, bnew_ref, lfnew_ref, sl_ref,
                    akv_hbm, bkv_hbm, lft_hbm, oa_ref, ob_ref,
                    abuf, bbuf, lbuf, sem_a, sem_b, sem_l, abf, bbf, *, n_pages):
    slot = _prefetch_schedule(pt_ref, n_pages, [(akv_hbm, abuf, _lane_window, sem_a),
                                                (bkv_hbm, bbuf, _lane_window, sem_b),
                                                (lft_hbm, lbuf, _row_window, sem_l)])
    past = n_pages * PAGE
    n_chunks = past // SAMP_CHUNK
    pages_per_chunk = SAMP_CHUNK // PAGE

    def attend(q16, kv_bf, n_k, new_row, bias_chunk, bias_new):
        k_rows, v_rows = slice(0, n_k), slice(n_k, 2 * n_k)
        lgs = []
        m = jnp.full((2 * N_HEADS, 1), NEG_INF, F32)
        for c in range(n_chunks):
            cs = slice(c * SAMP_CHUNK, (c + 1) * SAMP_CHUNK)
            lg = _dot(q16, kv_bf[k_rows, cs]) + bias_chunk(c)
            lgs.append(lg)
            m = jnp.maximum(m, jnp.max(lg, axis=1, keepdims=True))
        new_bf = new_row.astype(MXU_DTYPE).astype(F32)
        lg_new = jnp.sum(q16.astype(F32) * new_bf[:, k_rows], axis=1, keepdims=True) + bias_new
        m = jnp.maximum(m, lg_new)
        l = jnp.zeros((2 * N_HEADS, 1), F32)
        acc = jnp.zeros((2 * N_HEADS, n_k), F32)
        for c in range(n_chunks):
            cs = slice(c * SAMP_CHUNK, (c + 1) * SAMP_CHUNK)
            p = jnp.exp(lgs[c] - m)
            l = l + jnp.sum(p, axis=1, keepdims=True)
            acc = acc + _dot_nt(p.astype(MXU_DTYPE), kv_bf[v_rows, cs])
        p_new = jnp.exp(lg_new - m)
        l = l + p_new
        acc = acc + p_new.astype(MXU_DTYPE).astype(F32) * new_bf[:, v_rows]
        return acc / l

    for c in range(n_chunks):
        cs = slice(c * SAMP_CHUNK, (c + 1) * SAMP_CHUNK)
        abf[:, cs] = abuf[slot, :, cs].astype(abf.dtype)
    oa_ref[0] = attend(qa_ref[0], abf, HEAD_DIM, anew_ref[0],
                       lambda c: bias_ref[0, :, c * SAMP_CHUNK:(c + 1) * SAMP_CHUNK],
                       bias_ref[0, :, past:past + 1])

    lf = lbuf[slot]
    sl = sl_ref[...]
    hi, mid, lo = _split3(lf)
    rev = _dot(hi, sl) + _dot(mid, sl) + _dot(lo, sl)
    tot = rev[:, 0:1] + lf[:, 0:1]
    run = jnp.zeros((N_HEADS, 1), F32)
    rp = [None] * n_pages
    for p in reversed(range(n_pages)):
        ps = slice(p * N_HEADS, (p + 1) * N_HEADS)
        rp[p] = rev[ps] + run
        run = run + tot[ps]
    nq = lfnew_ref[0]
    zpad = jnp.zeros((N_HEADS, SAMP_CHUNK), F32)

    def bias_b(c):
        rpc = jnp.concatenate(rp[c * pages_per_chunk:(c + 1) * pages_per_chunk], axis=1)
        return jnp.concatenate([rpc, zpad], axis=0) + nq

    for c in range(n_chunks):
        cs = slice(c * SAMP_CHUNK, (c + 1) * SAMP_CHUNK)
        bbf[:, cs] = bbuf[slot, :, cs].astype(bbf.dtype)
    out_b = attend(qb_ref[0], bbf, 2 * HEAD_DIM, bnew_ref[0], bias_b, jnp.zeros((2 * N_HEADS, 1), F32))
    head = lax.broadcasted_iota(jnp.int32, out_b.shape, 0)
    second_group = head >= N_HEADS // B_KV_HEADS
    ob_ref[0] = jnp.where(second_group, pltpu.roll(out_b, HEAD_DIM, 1), out_b)


def _samp_attn(page_table, qa16, qb16, bias, a_new, b_new, lf_new16, cache_a, cache_b, cache_lft):
    db, n_pages = page_table.shape
    past = n_pages * PAGE
    sl = jnp.tril(jnp.ones((LANES, LANES), F32), k=-1).astype(MXU_DTYPE)
    body = functools.partial(_samp_attn_body, n_pages=n_pages)
    row = lambda b, pt: (b, 0, 0)
    any_spec = pl.BlockSpec(memory_space=pl.ANY)
    return pl.pallas_call(
        body,
        grid_spec=pltpu.PrefetchScalarGridSpec(
            num_scalar_prefetch=1, grid=(db,),
            in_specs=[pl.BlockSpec((1,) + qa16.shape[1:], row), pl.BlockSpec((1,) + qb16.shape[1:], row),
                      pl.BlockSpec((1,) + bias.shape[1:], row), pl.BlockSpec((1,) + a_new.shape[1:], row),
                      pl.BlockSpec((1,) + b_new.shape[1:], row), pl.BlockSpec((1,) + lf_new16.shape[1:], row),
                      pl.BlockSpec(sl.shape, lambda b, pt: (0, 0)), any_spec, any_spec, any_spec],
            out_specs=[pl.BlockSpec((1, 2 * N_HEADS, HEAD_DIM), row),
                       pl.BlockSpec((1, 2 * N_HEADS, 2 * HEAD_DIM), row)],
            scratch_shapes=[pltpu.VMEM((2, 2 * HEAD_DIM, past), F32), pltpu.VMEM((2, 4 * HEAD_DIM, past), F32),
                            pltpu.VMEM((2, n_pages * N_HEADS, PAGE), F32),
                            pltpu.SemaphoreType.DMA((2,)), pltpu.SemaphoreType.DMA((2,)),
                            pltpu.SemaphoreType.DMA((2,)),
                            pltpu.VMEM((2 * HEAD_DIM, past), MXU_DTYPE), pltpu.VMEM((4 * HEAD_DIM, past), MXU_DTYPE)]),
        out_shape=[jax.ShapeDtypeStruct((db, 2 * N_HEADS, HEAD_DIM), F32),
                   jax.ShapeDtypeStruct((db, 2 * N_HEADS, 2 * HEAD_DIM), F32)],
        compiler_params=pltpu.CompilerParams(dimension_semantics=("arbitrary",), vmem_limit_bytes=VMEM_LIMIT),
        name="samp_attn",
    )(page_table.reshape(-1), qa16, qb16, bias, a_new, b_new, lf_new16, sl, cache_a, cache_b, cache_lft)


def _ffn_body(x_ref, o_ref, ge_ref, be_ref, wo_ref, g1_ref, b1_ref, wg_ref, wu_ref, wd_ref, g2_ref, b2_ref,
              y_ref, *, alpha, ff_chunk):
    h = _layer_norm(x_ref[...], ge_ref[...], be_ref[...])
    mixed = _dot(o_ref[...], wo_ref[...])
    h1 = _layer_norm(alpha * h + mixed, g1_ref[...], b1_ref[...])
    h1b = h1.astype(MXU_DTYPE)
    d_ff = wg_ref.shape[1]
    ffn = jnp.zeros(h1.shape, F32)
    for c in range(d_ff // ff_chunk):
        cs = slice(c * ff_chunk, (c + 1) * ff_chunk)
        gate = _dot(h1b, wg_ref[:, cs])
        up = _dot(h1b, wu_ref[:, cs])
        act = (gate * jax.nn.sigmoid(gate)) * up
        ffn = ffn + _dot(act.astype(MXU_DTYPE), wd_ref[cs, :])
    y_ref[...] = _layer_norm(alpha * h1 + ffn, g2_ref[...], b2_ref[...])


def _ffn_chunk(d_ff):
    for n in (11, 8, 6, 4, 2, 1):
        if d_ff % (n * LANES) == 0:
            return n * LANES
    raise ValueError(f"d_ff={d_ff} is not a multiple of {LANES}")


def _ffn_call(x2, o2, ln_e, w_o, ln1, w_gate, w_up, w_down, ln2, alpha, tm):
    m, d = x2.shape
    assert m % tm == 0
    d_ff = w_gate.shape[1]
    row = lambda i: (i, 0)
    const = lambda i: (0, 0)
    vec = pl.BlockSpec((1, d), const)
    resident = lambda shape: pl.BlockSpec(shape, const, pipeline_mode=pl.Buffered(1))
    body = functools.partial(_ffn_body, alpha=alpha, ff_chunk=_ffn_chunk(d_ff))
    return pl.pallas_call(
        body,
        grid=(m // tm,),
        in_specs=[pl.BlockSpec((tm, d), row), pl.BlockSpec((tm, o2.shape[1]), row), vec, vec,
                  resident(w_o.shape), vec, vec, resident(w_gate.shape), resident(w_up.shape),
                  resident(w_down.shape), vec, vec],
        out_specs=pl.BlockSpec((tm, d), row),
        out_shape=jax.ShapeDtypeStruct((m, d), F32),
        compiler_params=pltpu.CompilerParams(dimension_semantics=("arbitrary",), vmem_limit_bytes=VMEM_LIMIT),
        name="ffn",
    )(x2, o2, ln_e[0].reshape(1, d), ln_e[1].reshape(1, d), w_o, ln1[0].reshape(1, d), ln1[1].reshape(1, d),
      w_gate, w_up, w_down, ln2[0].reshape(1, d), ln2[1].reshape(1, d))


def _row_tile(n, cap):
    t = min(n, cap)
    while n % t:
        t //= 2
    return t


def kernel(x_prompt, x_sample, cache_a_kv, cache_idx_k, cache_b_kv, cache_b_logf, page_table, meta,
           ln_emb_g, ln_emb_b, w_in, b_forget, w_o, ln1_g, ln1_b, w_gate, w_up, w_down, ln2_g, ln2_b):
    depth = w_in.shape[0]
    assert depth == 1, "single-layer trunk only"
    bsz, seq, d = x_prompt.shape
    db, ds, _ = x_sample.shape
    assert ds == 1, "one new token per decode sequence"
    n_meta = meta.shape[0]
    n_pages = page_table.shape[1]
    past = n_pages * PAGE
    alpha = (2.0 * depth) ** 0.25
    topk_p = min(TOPK_MAX, seq // 4)
    topk_s = min(TOPK_MAX, (past + ds) // 4)

    sizes = (512, 64, 64, 512, 64, 8, 512, 128, 128, 8)
    offs = [0]
    for sz in sizes:
        offs.append(offs[-1] + sz)
    w0 = w_in[0]
    part = lambda i: w0[:, offs[i]:offs[i + 1]]
    qa_w, ka_w, va_w, qi_w, ki_w, wi_w, qb_w, kb_w, vb_w, fl_w = [part(i) for i in range(10)]
    pad_w = jnp.zeros((d, _C_KVB - _C_KIW - HEAD_DIM - 2 * N_HEADS), w0.dtype)
    w_cat = jnp.concatenate([qa_w, qi_w, qb_w, ka_w, va_w, ki_w, wi_w, fl_w, pad_w, kb_w, vb_w],
                            axis=1).astype(MXU_DTYPE)
    zero_c = jnp.zeros((1, N_HEADS), F32)
    proj = functools.partial(_proj_call, w_cat=w_cat, ln_g=ln_emb_g, ln_b=ln_emb_b, b_forget=b_forget[0])

    (_, wc_m, kidx_m, kva_m, kvb_m, akv_m, idxk_m, bkv_m, lf_m, kext_m) = proj(
        meta[None].astype(x_prompt.dtype), jnp.arange(n_meta), zero_c, tm=n_meta)
    c_meta = wc_m[0, :, N_HEADS:]
    (q_t, wc, kidx, kva, kvb, akv, idxk, bkv, lf, kext) = proj(
        x_prompt, n_meta + jnp.arange(seq), c_meta[n_meta - 1:n_meta], tm=_row_tile(seq, 512), q_token_minor=True)
    (qcat_s, wc_s, _, _, _, akv_s, idxk_s, bkv_s, lf_s, _) = proj(
        x_sample.reshape(1, db, d), jnp.full((db,), past, jnp.int32), zero_c, tm=_row_tile(db, 512))

    pad_rows = lambda t: jnp.pad(t[0], ((0, LANES - n_meta), (0, 0)))
    t_last = lambda t: jnp.swapaxes(t, -1, -2)
    vat, vbt = t_last(kva[:, :, HEAD_DIM:]), t_last(kvb[:, :, 2 * HEAD_DIM:])
    vat_m, vbt_m = t_last(pad_rows(kva_m)[:, HEAD_DIM:]), t_last(pad_rows(kvb_m)[:, 2 * HEAD_DIM:])
    o_t = _prompt_attn(q_t, t_last(wc), kidx, kva, kvb, vat, vbt, kext,
                       pad_rows(kidx_m), pad_rows(kva_m), pad_rows(kvb_m), vat_m, vbt_m, pad_rows(kext_m),
                       n_meta, topk_p)
    o_prompt = t_last(o_t)

    qs = qcat_s[0].astype(F32).reshape(db, 3, N_HEADS, HEAD_DIM)
    pad_heads = lambda t: jnp.pad(t, ((0, 0), (0, N_HEADS), (0, 0)))
    qa16 = pad_heads(qs[:, 0]).astype(MXU_DTYPE)
    qi16 = pad_heads(qs[:, 1]).astype(MXU_DTYPE)
    half = N_HEADS // B_KV_HEADS
    head_group = jnp.arange(N_HEADS)[:, None] // half
    lane_group = jnp.arange(2 * HEAD_DIM)[None, :] // HEAD_DIM
    qb16 = pad_heads(jnp.where(head_group == lane_group, jnp.concatenate([qs[:, 2], qs[:, 2]], axis=2),
                               0.0)).astype(MXU_DTYPE)
    wi16 = pad_heads(wc_s[0, :, :N_HEADS, None])
    lfn16 = pad_heads(lf_s[0][:, :, None])
    feat_major = lambda c: jnp.moveaxis(c[0], 1, -1).reshape(c.shape[1], -1, PAGE)
    scores = _samp_scores(page_table, qi16, wi16, idxk_s[0][:, None, :], feat_major(cache_idx_k))
    n_t = n_pages + 1
    scores_tt = jnp.transpose(scores.reshape(db, n_t, PAGE), (1, 2, 0))
    selb_tt = _samp_select(scores_tt, n_pages, topk_s)
    bias = jnp.transpose(selb_tt, (2, 0, 1)).reshape(db, 1, n_t * PAGE)
    oa_s, ob_s = _samp_attn(page_table, qa16, qb16, bias, akv_s[0][:, None, :], bkv_s[0][:, None, :], lfn16,
                            feat_major(cache_a_kv), feat_major(cache_b_kv), feat_major(cache_b_logf))
    heads_flat = lambda t: t[:, :N_HEADS, :HEAD_DIM].reshape(db, N_HEADS * HEAD_DIM)
    o_sample = jnp.concatenate([heads_flat(oa_s), heads_flat(ob_s)], axis=1).astype(MXU_DTYPE)

    cast = lambda w: w[0].astype(MXU_DTYPE)
    ffn = functools.partial(_ffn_call, ln_e=(ln_emb_g, ln_emb_b), w_o=cast(w_o), ln1=(ln1_g[0], ln1_b[0]),
                            w_gate=cast(w_gate), w_up=cast(w_up), w_down=cast(w_down), ln2=(ln2_g[0], ln2_b[0]),
                            alpha=alpha)
    y_prompt = ffn(x_prompt.reshape(bsz * seq, d), o_prompt.reshape(bsz * seq, -1),
                   tm=_row_tile(bsz * seq, 512)).reshape(bsz, seq, d)
    y_sample = ffn(x_sample.reshape(db, d), o_sample, tm=_row_tile(db, 512)).reshape(db, ds, d)

    def with_meta(m_part, x_part, tail):
        m_b = jnp.broadcast_to(m_part, (bsz,) + m_part.shape[1:])
        return jnp.concatenate([m_b, x_part], axis=1).reshape((1, bsz, n_meta + seq) + tail)

    return (y_prompt, y_sample,
            with_meta(akv_m, akv, (2, 1, HEAD_DIM)),
            with_meta(idxk_m, idxk, (HEAD_DIM,)),
            with_meta(bkv_m, bkv, (2, B_KV_HEADS, HEAD_DIM)),
            with_meta(lf_m, lf, (N_HEADS,)),
            akv_s[0].reshape(1, db, ds, 2, 1, HEAD_DIM),
            idxk_s[0].reshape(1, db, ds, HEAD_DIM),
            bkv_s[0].reshape(1, db, ds, 2, B_KV_HEADS, HEAD_DIM),
            lf_s[0].reshape(1, db, ds, N_HEADS))
```

```python
import functools

import jax
import jax.numpy as jnp
from jax import lax
from jax.experimental import pallas as pl
from jax.experimental.pallas import tpu as pltpu

HEAD_DIM = 64
N_HEADS = 8
B_KV_HEADS = 2
ROPE_HALF = 8
ROPE_THETA = 500000.0
TOPK_MAX = 256
PAGE = 128
LN_EPS = 1e-5
Q_BLOCK = 128
KEY_TILE = 256
LANES = 128
SUBLANES = 8
RADIX_STEPS = 4
ONES_ROWS = 16
C_ONES_AT = 32
INT_MIN = -2 ** 31
NEG_INF = float("-inf")
VMEM_LIMIT = 56 * 1024 * 1024

F32 = jnp.float32
BF16 = jnp.bfloat16
MXU_DTYPE = BF16

_NT = (((1,), (1,)), ((), ()))


def _dot(a, b):
    return jnp.dot(a, b, preferred_element_type=F32)


def _dot_nt(a, b):
    return lax.dot_general(a, b, _NT, preferred_element_type=F32)


def _layer_norm(x, g, b):
    mu = jnp.mean(x, axis=-1, keepdims=True)
    xc = x - mu
    var = jnp.mean(xc * xc, axis=-1, keepdims=True)
    return xc * lax.rsqrt(var + LN_EPS) * g + b


def _split3(x):
    hi = x.astype(MXU_DTYPE)
    r1 = x - hi.astype(F32)
    mid = r1.astype(MXU_DTYPE)
    lo = (r1 - mid.astype(F32)).astype(MXU_DTYPE)
    return hi, mid, lo


_C_QA, _C_QI, _C_QB, _C_KVA, _C_KIW, _C_KVB, _C_END = 0, 512, 1024, 1536, 1664, 1792, 2048


def _rope(v, cos, sa, sb):
    return v * cos + pltpu.roll(v, LANES - ROPE_HALF, 1) * sa + pltpu.roll(v, ROPE_HALF, 1) * sb


def _proj_body(x_ref, g_ref, b_ref, w_ref, bf_ref, cos1_ref, sa1_ref, sb1_ref, cos2_ref, sa2_ref, sb2_ref,
               tri_ref, c0_ref,
               q_ref, wc_ref, kidx_ref, kva_ref, kvb_ref, akv_ref, idxk_ref, bkv_ref, lf_ref, kext_ref,
               carry_ref, *, q_token_minor):
    j = pl.program_id(1)
    tm = x_ref.shape[1]
    h = _layer_norm(x_ref[0], g_ref[...], b_ref[...])
    proj = _dot(h.astype(MXU_DTYPE), w_ref[...])

    cos1, sa1, sb1 = cos1_ref[...], sa1_ref[...], sb1_ref[...]
    scale = HEAD_DIM ** -0.5
    for g in range(_C_KVA // LANES):
        gs = slice(g * LANES, (g + 1) * LANES)
        seg = proj[:, gs]
        seg = (_rope(seg, cos1, sa1, sb1) if g < _C_QB // LANES else seg) * scale
        if q_token_minor:
            q_ref[0, gs, :] = seg.T.astype(q_ref.dtype)
        else:
            q_ref[0, :, gs] = seg.astype(q_ref.dtype)

    cos2, sa2, sb2 = cos2_ref[...], sa2_ref[...], sb2_ref[...]
    kva = _rope(proj[:, _C_KVA:_C_KIW], cos2, sa2, sb2)
    akv_ref[0] = kva
    kva_ref[0] = kva.astype(kva_ref.dtype)
    kiw = _rope(proj[:, _C_KIW:_C_KVB], cos2, sa2, sb2)
    idxk_ref[0] = kiw[:, :HEAD_DIM]
    kidx_ref[0] = kiw[:, :HEAD_DIM].astype(kidx_ref.dtype)
    wi = kiw[:, HEAD_DIM:HEAD_DIM + N_HEADS] * (N_HEADS ** -0.5)
    z = kiw[:, HEAD_DIM + N_HEADS:HEAD_DIM + 2 * N_HEADS] + bf_ref[...]
    lf = jnp.minimum(z, 0.0) - jnp.log1p(jnp.exp(-jnp.abs(z)))
    lf_ref[0] = lf
    kvb = proj[:, _C_KVB:_C_END]
    bkv_ref[0] = kvb
    kvb_ref[0] = kvb.astype(kvb_ref.dtype)

    @pl.when(j == 0)
    def _():
        carry_ref[...] = c0_ref[...]

    tri = tri_ref[...]
    hi, mid, lo = _split3(lf)
    cs = _dot(tri, hi) + _dot(tri, mid) + _dot(tri, lo) + carry_ref[...]
    carry_ref[...] = cs[tm - 1:tm, :]
    wc_ref[0] = jnp.concatenate([wi, cs], axis=1)
    pieces = [-p.astype(F32) for p in _split3(cs)]
    lane = lax.broadcasted_iota(jnp.int32, (tm, LANES - C_ONES_AT), 1)
    ones_at = jnp.where(lane < 3, 1.0, 0.0)
    kext_ref[0] = jnp.concatenate(pieces + [jnp.zeros((tm, C_ONES_AT - 3 * N_HEADS), F32), ones_at],
                                  axis=1).astype(kext_ref.dtype)


def _rope_tables(pos):
    inv = ROPE_THETA ** (-jnp.arange(ROPE_HALF, dtype=F32) / ROPE_HALF)
    ang = pos.astype(F32)[:, None] * inv
    cos, sin = jnp.cos(ang), jnp.sin(ang)
    ones = jnp.ones((pos.shape[0], HEAD_DIM - 2 * ROPE_HALF), F32)
    zeros = jnp.zeros_like(ones)
    zh = jnp.zeros_like(sin)
    cos_h = jnp.concatenate([cos, cos, ones], 1)
    sa_h = jnp.concatenate([-sin, zh, zeros], 1)
    sb_h = jnp.concatenate([zh, sin, zeros], 1)
    one_h, zero_h = jnp.ones_like(cos_h), jnp.zeros_like(cos_h)
    t1 = [jnp.concatenate([t, t], 1) for t in (cos_h, sa_h, sb_h)]
    t2 = [jnp.concatenate([cos_h, one_h], 1), jnp.concatenate([sa_h, zero_h], 1),
          jnp.concatenate([sb_h, zero_h], 1)]
    return t1 + t2


def _proj_call(x3, pos, c0, w_cat, ln_g, ln_b, b_forget, tm, q_token_minor=False):
    nb, s, d = x3.shape
    assert s % tm == 0
    tables = _rope_tables(pos)
    tri = jnp.tril(jnp.ones((tm, tm), F32)).astype(MXU_DTYPE)
    row = lambda b, j: (b, j, 0)
    const2 = lambda b, j: (0, 0)
    tab_spec = pl.BlockSpec((tm, LANES), lambda b, j: (j, 0))
    widths = (3 * 512, 2 * N_HEADS, HEAD_DIM, 2 * HEAD_DIM, 4 * HEAD_DIM, 2 * HEAD_DIM, HEAD_DIM, 4 * HEAD_DIM,
              N_HEADS, LANES)
    dtypes = (MXU_DTYPE, F32, MXU_DTYPE, MXU_DTYPE, MXU_DTYPE, F32, F32, F32, F32, MXU_DTYPE)
    out_specs = [pl.BlockSpec((1, tm, w), row) for w in widths]
    out_shape = [jax.ShapeDtypeStruct((nb, s, w), dt) for w, dt in zip(widths, dtypes)]
    if q_token_minor:
        out_specs[0] = pl.BlockSpec((1, widths[0], tm), lambda b, j: (b, 0, j))
        out_shape[0] = jax.ShapeDtypeStruct((nb, widths[0], s), dtypes[0])
    return pl.pallas_call(
        functools.partial(_proj_body, q_token_minor=q_token_minor),
        grid=(nb, s // tm),
        in_specs=[pl.BlockSpec((1, tm, d), row),
                  pl.BlockSpec((1, d), const2), pl.BlockSpec((1, d), const2),
                  pl.BlockSpec(w_cat.shape, const2),
                  pl.BlockSpec((1, N_HEADS), const2)] + [tab_spec] * 6 +
                 [pl.BlockSpec((tm, tm), const2), pl.BlockSpec((1, N_HEADS), const2)],
        out_specs=out_specs,
        out_shape=out_shape,
        scratch_shapes=[pltpu.VMEM((1, N_HEADS), F32)],
        compiler_params=pltpu.CompilerParams(dimension_semantics=("arbitrary", "arbitrary"),
                                             vmem_limit_bytes=VMEM_LIMIT),
        name="proj",
    )(x3, ln_g.reshape(1, d), ln_b.reshape(1, d), w_cat, b_forget.reshape(1, N_HEADS), *tables, tri, c0)


def _ordered_int_to_float(key):
    return pltpu.bitcast(key ^ ((key >> 31) & jnp.int32(0x7FFFFFFF)), F32)


def _select_topk(sc_ref, selb_ref, lt_ref, n_tiles, k):
    rows = sc_ref.shape[1]
    chains = 4
    part = chains * SUBLANES
    kf = float(k)

    def count(pred):
        def one(t, acc):
            hit = jnp.where(pred(sc_ref[t]), 1.0, 0.0)
            return acc + jnp.sum(hit.reshape(rows // part, part, LANES), axis=0)

        def pair(i, acc):
            return one(2 * i + 1, one(2 * i, acc))
        acc = lax.fori_loop(0, n_tiles // 2, pair, jnp.zeros((part, LANES), F32))
        acc = lax.cond(n_tiles % 2 == 1, lambda a: one(n_tiles - 1, a), lambda a: a, acc)
        return jnp.sum(acc, axis=0, keepdims=True)

    def n_unsettled(cnt):
        return jnp.sum(jnp.where(cnt != kf, 1.0, 0.0))

    n_adm = count(lambda x: x > NEG_INF)
    small = n_adm <= kf
    cur0 = jnp.full((1, LANES), INT_MIN, jnp.int32)
    cnt0 = jnp.where(small, kf, n_adm)

    def cond(st):
        it, _, _, bad = st
        return jnp.logical_and(it < 32 // RADIX_STEPS, bad > 0.0)

    def body(st):
        it, cur, cnt, _ = st
        for step in range(RADIX_STEPS):
            bit = 31 - (it * RADIX_STEPS + step)
            cand = cur + jnp.left_shift(jnp.int32(1), bit)
            thr = _ordered_int_to_float(cand)
            c = count(lambda x: x >= thr)
            take = c >= kf
            cur = jnp.where(take, cand, cur)
            cnt = jnp.where(take, c, cnt)
        return it + 1, cur, cnt, n_unsettled(cnt)

    _, cur, cnt, _ = lax.while_loop(cond, body, (jnp.int32(0), cur0, cnt0, n_unsettled(cnt0)))
    has_ties = jnp.sum(jnp.where(cnt > kf, 1.0, 0.0)) > 0.0
    thr = _ordered_int_to_float(jnp.where(small, INT_MIN + (1 << 23), cur))

    @pl.when(jnp.logical_not(has_ties))
    def _():
        def body(t, carry):
            selb_ref[t] = jnp.where(sc_ref[t] >= thr, 0.0, NEG_INF)
            return carry
        lax.fori_loop(0, n_tiles, body, 0)

    @pl.when(has_ties)
    def _():
        need = kf - count(lambda x: x > thr)
        lt = lt_ref[...]

        def body(t, seen):
            st = sc_ref[t]
            eqf = jnp.where(st == thr, 1.0, 0.0)
            rank = _dot(lt, eqf.astype(MXU_DTYPE)) + seen
            keep_tie = jnp.where(rank <= need, eqf, 0.0)
            selb_ref[t] = jnp.where(st > thr, 0.0, jnp.where(keep_tie > 0.0, 0.0, NEG_INF))
            return seen + jnp.sum(eqf, axis=0, keepdims=True)
        lax.fori_loop(0, n_tiles, body, jnp.zeros((1, LANES), F32))


def _prompt_attn_body(qt_ref, wct_ref, kidx_ref, kva_ref, kvb_ref, vat_ref, vbt_ref, kext_ref,
                      kidxm_ref, kvam_ref, kvbm_ref, vatm_ref, vbtm_ref, kextm_ref, lt_ref, sel_ref,
                      o_ref,
                      kidx_c, kva_c, kb_c, vat_c, vbt_c, keys_ref, cmask_ref, selb_ref,
                      lga_ref, lgb_ref, acca_ref, accb_ref, *, n_meta, topk):
    j = pl.program_id(1)
    s = kidx_ref.shape[1]
    kp = kidx_c.shape[0]
    n_u = keys_ref.shape[0]
    cols = N_HEADS * Q_BLOCK
    groups = KEY_TILE // SUBLANES

    @pl.when(j == 0)
    def _():
        tail = kp - LANES - s
        for dst, lane0, m_src, x_src, width in ((kidx_c, 0, kidxm_ref, kidx_ref, HEAD_DIM),
                                                (kva_c, 0, kvam_ref, kva_ref, 2 * HEAD_DIM),
                                                (kb_c, 0, kvbm_ref, kvb_ref, 2 * HEAD_DIM),
                                                (kb_c, 2 * HEAD_DIM, kextm_ref, kext_ref, LANES)):
            lanes = slice(lane0, lane0 + width)
            dst[0:LANES, lanes] = m_src[:, 0:width]
            dst[LANES:LANES + s, lanes] = x_src[0, :, 0:width]
            if tail:
                dst[LANES + s:, lanes] = jnp.zeros((tail, width), dst.dtype)
        ones = jnp.ones((ONES_ROWS, kp), vat_c.dtype)
        for dst, m_src, x_src in ((vat_c, vatm_ref, vat_ref), (vbt_c, vbtm_ref, vbt_ref)):
            full = jnp.concatenate([m_src[...], x_src[0], jnp.zeros((m_src.shape[0], tail), dst.dtype)], axis=1)
            parts = []
            for g in range(m_src.shape[0] // HEAD_DIM):
                parts += [full[g * HEAD_DIM:(g + 1) * HEAD_DIM], ones]
            full = jnp.concatenate(parts, axis=0)
            for u in range(n_u):
                dst[u] = full[:, u * KEY_TILE:(u + 1) * KEY_TILE]

    qt = qt_ref[0]

    def heads_on_lanes(off):
        return jnp.concatenate([qt[off + h * HEAD_DIM:off + (h + 1) * HEAD_DIM, :] for h in range(N_HEADS)], axis=1)

    qa_t, qi_t, qb_t = heads_on_lanes(_C_QA), heads_on_lanes(_C_QI), heads_on_lanes(_C_QB)
    z_all = jnp.zeros((HEAD_DIM, cols), qt.dtype)
    z_half = jnp.zeros((HEAD_DIM, cols // 2), qt.dtype)
    qa_pad = jnp.concatenate([qa_t, z_all], axis=0)
    qb_bd = jnp.concatenate([jnp.concatenate([qb_t[:, :cols // 2], z_half], axis=1),
                             jnp.concatenate([z_half, qb_t[:, cols // 2:]], axis=1)], axis=0)
    wct = wct_ref[0]
    cq = jnp.concatenate([wct[N_HEADS + h:N_HEADS + h + 1, :] for h in range(N_HEADS)], axis=1)
    row = lax.broadcasted_iota(jnp.int32, (ONES_ROWS, cols), 0)
    cq_rows = jnp.zeros((ONES_ROWS, cols), F32)
    for i, piece in enumerate(_split3(cq)):
        cq_rows = jnp.where(row == i, piece.astype(F32), cq_rows)
    qb_ext = jnp.concatenate([qb_bd, sel_ref[...], cq_rows.astype(qt.dtype),
                              jnp.zeros((LANES - C_ONES_AT - ONES_ROWS, cols), qt.dtype)], axis=0)

    per = KEY_TILE // LANES
    n_tiles = jnp.right_shift(j + 1 + per, per.bit_length() - 1)

    def tile_rows(u):
        return pl.ds(pl.multiple_of(u * KEY_TILE, KEY_TILE), KEY_TILE)

    def pass1(q_t, k_tile, bias, lg_ref):
        def step(u, m):
            lg = _dot(k_tile(u), q_t)
            ms = []
            for h in range(N_HEADS):
                hs = slice(h * Q_BLOCK, (h + 1) * Q_BLOCK)
                lgh = lg[:, hs] + bias(u, h)
                lg_ref[u, :, hs] = lgh
                ms.append(jnp.max(lgh.reshape(groups, SUBLANES, Q_BLOCK), axis=0))
            return jnp.maximum(m, jnp.concatenate(ms, axis=1))
        return step

    def pass2(lg_ref, m_row, pv):
        def step(u):
            p = jnp.exp(lg_ref[u] - m_row)
            pv(u, p.astype(MXU_DTYPE))
        return step

    def col_max(m):
        return jnp.max(m, axis=0, keepdims=True)

    m_init = jnp.full((SUBLANES, cols), NEG_INF, F32)
    v_rows = HEAD_DIM + ONES_ROWS

    def pv_a(u, p):
        acca_ref[...] += _dot(vat_c[u], p)

    def pv_b(u, p):
        vt = vbt_c[u]
        accb_ref[:, 0:cols // 2] += _dot(vt[0:v_rows], p[:, 0:cols // 2])
        accb_ref[:, cols // 2:] += _dot(vt[v_rows:], p[:, cols // 2:])

    def normalised(acc_ref):
        return acc_ref[0:HEAD_DIM, :] / acc_ref[HEAD_DIM:HEAD_DIM + 1, :]

    a_pass1 = pass1(qa_pad, lambda u: kva_c[tile_rows(u), :], lambda u, h: selb_ref[u], lga_ref)
    b_pass1 = pass1(qb_ext, lambda u: kb_c[tile_rows(u), :], lambda u, h: cmask_ref[u], lgb_ref)

    def idx_step(u):
        d = _dot(kidx_c[tile_rows(u), :], qi_t)
        sc = jnp.zeros((KEY_TILE, Q_BLOCK), F32)
        for h in range(N_HEADS):
            sc = sc + wct[h:h + 1, :] * jnp.maximum(d[:, h * Q_BLOCK:(h + 1) * Q_BLOCK], 0.0)
        key_i = u * KEY_TILE + lax.broadcasted_iota(jnp.int32, (KEY_TILE, Q_BLOCK), 0)
        qry_i = lax.broadcasted_iota(jnp.int32, (KEY_TILE, Q_BLOCK), 1)
        valid = jnp.logical_or(key_i < n_meta,
                               jnp.logical_and(key_i >= LANES, key_i - LANES <= j * Q_BLOCK + qry_i))
        keys_ref[u] = jnp.where(valid, sc, NEG_INF)
        cmask_ref[u] = jnp.where(valid, 0.0, NEG_INF)

    def over_tiles(step, carry):
        def pair(i, c):
            return step(2 * i + 1, step(2 * i, c))
        return lax.fori_loop(0, jnp.right_shift(n_tiles + 1, 1), pair, carry)

    def idx_and_b1(u, m_b):
        idx_step(u)
        return b_pass1(u, m_b)
    m_b = col_max(over_tiles(idx_and_b1, m_init))

    _select_topk(keys_ref, selb_ref, lt_ref, n_tiles, topk)

    @pl.when(jnp.bitwise_and(n_tiles, 1) == 1)
    def _():
        selb_ref[n_tiles] = jnp.full((KEY_TILE, Q_BLOCK), NEG_INF, F32)

    acca_ref[...] = jnp.zeros(acca_ref.shape, F32)
    accb_ref[...] = jnp.zeros(accb_ref.shape, F32)
    b_pass2 = pass2(lgb_ref, m_b, pv_b)

    def a1_and_b2(u, m_a):
        b_pass2(u)
        return a_pass1(u, m_a)
    m_a = col_max(over_tiles(a1_and_b2, m_init))
    a_pass2 = pass2(lga_ref, m_a, pv_a)

    def a2(u, carry):
        a_pass2(u)
        return carry
    over_tiles(a2, 0)

    out_a = normalised(acca_ref)
    for h in range(N_HEADS):
        o_ref[0, h * HEAD_DIM:(h + 1) * HEAD_DIM, :] = out_a[:, h * Q_BLOCK:(h + 1) * Q_BLOCK].astype(o_ref.dtype)
    out_b = normalised(accb_ref)
    for h in range(N_HEADS):
        o_ref[0, (N_HEADS + h) * HEAD_DIM:(N_HEADS + h + 1) * HEAD_DIM, :] = (
            out_b[:, h * Q_BLOCK:(h + 1) * Q_BLOCK].astype(o_ref.dtype))


def _prompt_attn(qt, wct, kidx, kva, kvb, vat, vbt, kext, kidx_m, kva_m, kvb_m, vat_m, vbt_m, kext_m, n_meta,
                 topk):
    nb, _, s = qt.shape
    assert s % Q_BLOCK == 0
    nq = s // Q_BLOCK
    per = KEY_TILE // LANES
    n_u = (nq + per) // per
    n_u += n_u % 2
    kp = n_u * KEY_TILE
    cols = N_HEADS * Q_BLOCK
    v_rows = HEAD_DIM + ONES_ROWS
    lt = jnp.tril(jnp.ones((KEY_TILE, KEY_TILE), F32)).astype(MXU_DTYPE)
    sel_row, sel_col = jnp.arange(C_ONES_AT)[:, None], jnp.arange(cols)[None, :]
    sel = jnp.where((sel_row < 3 * N_HEADS) & (sel_row % N_HEADS == sel_col // Q_BLOCK), 1.0, 0.0).astype(MXU_DTYPE)
    qblk = lambda b, j: (b, 0, j)
    per_b = lambda b, j: (b, 0, 0)
    const2 = lambda b, j: (0, 0)
    whole = lambda a: pl.BlockSpec(a.shape, const2)
    body = functools.partial(_prompt_attn_body, n_meta=n_meta, topk=topk)
    return pl.pallas_call(
        body,
        grid=(nb, nq),
        in_specs=[pl.BlockSpec((1, qt.shape[1], Q_BLOCK), qblk),
                  pl.BlockSpec((1, wct.shape[1], Q_BLOCK), qblk)] +
                 [pl.BlockSpec((1,) + a.shape[1:], per_b) for a in (kidx, kva, kvb, vat, vbt, kext)] +
                 [whole(a) for a in (kidx_m, kva_m, kvb_m, vat_m, vbt_m, kext_m, lt, sel)],
        out_specs=pl.BlockSpec((1, 2 * N_HEADS * HEAD_DIM, Q_BLOCK), qblk),
        out_shape=jax.ShapeDtypeStruct((nb, 2 * N_HEADS * HEAD_DIM, s), MXU_DTYPE),
        scratch_shapes=[pltpu.VMEM((kp, HEAD_DIM), MXU_DTYPE),
                        pltpu.VMEM((kp, 2 * HEAD_DIM), MXU_DTYPE),
                        pltpu.VMEM((kp, 2 * HEAD_DIM + LANES), MXU_DTYPE),
                        pltpu.VMEM((n_u, v_rows, KEY_TILE), MXU_DTYPE),
                        pltpu.VMEM((n_u, B_KV_HEADS * v_rows, KEY_TILE), MXU_DTYPE),
                        pltpu.VMEM((n_u, KEY_TILE, Q_BLOCK), F32),
                        pltpu.VMEM((n_u, KEY_TILE, Q_BLOCK), F32),
                        pltpu.VMEM((n_u, KEY_TILE, Q_BLOCK), F32),
                        pltpu.VMEM((n_u, KEY_TILE, cols), F32),
                        pltpu.VMEM((n_u, KEY_TILE, cols), F32),
                        pltpu.VMEM((v_rows, cols), F32),
                        pltpu.VMEM((v_rows, cols), F32)],
        compiler_params=pltpu.CompilerParams(dimension_semantics=("arbitrary", "arbitrary"),
                                             vmem_limit_bytes=VMEM_LIMIT),
        name="prompt_attn",
    )(qt, wct, kidx, kva, kvb, vat, vbt, kext, kidx_m, kva_m, kvb_m, vat_m, vbt_m, kext_m, lt, sel)


def _lane_window(buf, slot, p):
    return buf.at[slot, :, pl.ds(pl.multiple_of(p * PAGE, PAGE), PAGE)]


def _row_window(buf, slot, p):
    rows = N_HEADS
    return buf.at[slot, pl.ds(pl.multiple_of(p * rows, rows), rows), :]


def _page_copy(stream, page, slot, p):
    src_hbm, buf, window, sem = stream
    return pltpu.make_async_copy(src_hbm.at[page], window(buf, slot, p), sem.at[slot])


def _start_pages(pt_ref, sample, slot, n_pages, streams):
    def one(p, carry):
        page = pt_ref[sample * n_pages + p]
        for stream in streams:
            _page_copy(stream, page, slot, p).start()
        return carry
    lax.fori_loop(0, n_pages, one, 0, unroll=PAGE_UNROLL)


def _wait_pages(slot, n_pages, streams):
    def one(p, carry):
        for stream in streams:
            _page_copy(stream, 0, slot, p).wait()
        return carry
    lax.fori_loop(0, n_pages, one, 0, unroll=PAGE_UNROLL)


def _prefetch_schedule(pt_ref, n_pages, streams):
    b = pl.program_id(0)
    nb = pl.num_programs(0)
    slot = lax.rem(b, 2)

    @pl.when(b == 0)
    def _():
        _start_pages(pt_ref, 0, 0, n_pages, streams)

    @pl.when(b + 1 < nb)
    def _():
        _start_pages(pt_ref, b + 1, 1 - slot, n_pages, streams)

    _wait_pages(slot, n_pages, streams)
    return slot


SAMP_CHUNK = 1024
PAGE_UNROLL = 8


def _samp_scores_body(pt_ref, qi_ref, wi_ref, knew_ref, cache_hbm, out_ref, buf, sem, *, n_pages):
    slot = _prefetch_schedule(pt_ref, n_pages, [(cache_hbm, buf, _lane_window, sem)])
    past = n_pages * PAGE
    qi = qi_ref[0]
    w = wi_ref[0]
    for c in range(past // SAMP_CHUNK):
        cs = slice(c * SAMP_CHUNK, (c + 1) * SAMP_CHUNK)
        kt = buf[slot, :, cs].astype(MXU_DTYPE)
        d = _dot(qi, kt)
        out_ref[0, :, cs] = jnp.sum(w * jnp.maximum(d, 0.0), axis=0, keepdims=True)
    knew = knew_ref[0].astype(MXU_DTYPE).astype(F32)
    d_new = jnp.sum(qi.astype(F32) * knew, axis=1, keepdims=True)
    s_new = jnp.sum(w * jnp.maximum(d_new, 0.0), axis=0, keepdims=True)
    out_ref[0, :, past:past + LANES] = jnp.broadcast_to(s_new, (1, LANES))


def _samp_scores(page_table, qi16, wi16, k_new, cache_idx_k):
    db, n_pages = page_table.shape
    past = n_pages * PAGE
    assert past % SAMP_CHUNK == 0
    body = functools.partial(_samp_scores_body, n_pages=n_pages)
    row = lambda b, pt: (b, 0, 0)
    return pl.pallas_call(
        body,
        grid_spec=pltpu.PrefetchScalarGridSpec(
            num_scalar_prefetch=1, grid=(db,),
            in_specs=[pl.BlockSpec((1,) + qi16.shape[1:], row), pl.BlockSpec((1,) + wi16.shape[1:], row),
                      pl.BlockSpec((1, 1, HEAD_DIM), row), pl.BlockSpec(memory_space=pl.ANY)],
            out_specs=pl.BlockSpec((1, 1, past + LANES), row),
            scratch_shapes=[pltpu.VMEM((2, HEAD_DIM, past), F32), pltpu.SemaphoreType.DMA((2,))]),
        out_shape=jax.ShapeDtypeStruct((db, 1, past + LANES), F32),
        compiler_params=pltpu.CompilerParams(dimension_semantics=("arbitrary",), vmem_limit_bytes=VMEM_LIMIT),
        name="samp_scores",
    )(page_table.reshape(-1), qi16, wi16, k_new, cache_idx_k)


def _samp_select_body(sc_ref, lt_ref, selb_ref, keys_ref, *, n_pages, topk):
    n_t = keys_ref.shape[0]
    row = lax.broadcasted_iota(jnp.int32, (PAGE, LANES), 0)
    for t in range(n_t):
        key = sc_ref[t]
        if t == n_pages:
            key = jnp.where(row == 0, key, NEG_INF)
        keys_ref[t] = key
    _select_topk(keys_ref, selb_ref, lt_ref, n_t, topk)


def _samp_select(scores_tt, n_pages, topk):
    n_t, _, db = scores_tt.shape
    assert db % LANES == 0 and n_t == n_pages + 1
    lt = jnp.tril(jnp.ones((PAGE, PAGE), F32)).astype(MXU_DTYPE)
    body = functools.partial(_samp_select_body, n_pages=n_pages, topk=topk)
    blk = pl.BlockSpec((n_t, PAGE, LANES), lambda r: (0, 0, r))
    return pl.pallas_call(
        body,
        grid=(db // LANES,),
        in_specs=[blk, pl.BlockSpec(lt.shape, lambda r: (0, 0))],
        out_specs=blk,
        out_shape=jax.ShapeDtypeStruct(scores_tt.shape, F32),
        scratch_shapes=[pltpu.VMEM((n_t, PAGE, LANES), F32)],
        compiler_params=pltpu.CompilerParams(dimension_semantics=("arbitrary",), vmem_limit_bytes=VMEM_LIMIT),
        name="samp_select",
    )(scores_tt, lt)


def _samp_attn_body(pt_ref, qa_ref, qb_ref, bias_ref, anew_ref, bnew_ref, lfnew_ref, sl_ref,
                    akv_hbm, bkv_hbm, lft_hbm, oa_ref, ob_ref,
                    abuf, bbuf, lbuf, sem_a, sem_b, sem_l, abf, bbf, *, n_pages):
    slot = _prefetch_schedule(pt_ref, n_pages, [(akv_hbm, abuf, _lane_window, sem_a),
                                                (bkv_hbm, bbuf, _lane_window, sem_b),
                                                (lft_hbm, lbuf, _row_window, sem_l)])
    past = n_pages * PAGE
    n_chunks = past // SAMP_CHUNK
    pages_per_chunk = SAMP_CHUNK // PAGE

    def attend(q16, kv_bf, n_k, new_row, bias_chunk, bias_new):
        k_rows, v_rows = slice(0, n_k), slice(n_k, 2 * n_k)
        lgs = []
        m = jnp.full((2 * N_HEADS, 1), NEG_INF, F32)
        for c in range(n_chunks):
            cs = slice(c * SAMP_CHUNK, (c + 1) * SAMP_CHUNK)
            lg = _dot(q16, kv_bf[k_rows, cs]) + bias_chunk(c)
            lgs.append(lg)
            m = jnp.maximum(m, jnp.max(lg, axis=1, keepdims=True))
        new_bf = new_row.astype(MXU_DTYPE).astype(F32)
        lg_new = jnp.sum(q16.astype(F32) * new_bf[:, k_rows], axis=1, keepdims=True) + bias_new
        m = jnp.maximum(m, lg_new)
        l = jnp.zeros((2 * N_HEADS, 1), F32)
        acc = jnp.zeros((2 * N_HEADS, n_k), F32)
        for c in range(n_chunks):
            cs = slice(c * SAMP_CHUNK, (c + 1) * SAMP_CHUNK)
            p = jnp.exp(lgs[c] - m)
            l = l + jnp.sum(p, axis=1, keepdims=True)
            acc = acc + _dot_nt(p.astype(MXU_DTYPE), kv_bf[v_rows, cs])
        p_new = jnp.exp(lg_new - m)
        l = l + p_new
        acc = acc + p_new.astype(MXU_DTYPE).astype(F32) * new_bf[:, v_rows]
        return acc / l

    for c in range(n_chunks):
        cs = slice(c * SAMP_CHUNK, (c + 1) * SAMP_CHUNK)
        abf[:, cs] = abuf[slot, :, cs].astype(abf.dtype)
    oa_ref[0] = attend(qa_ref[0], abf, HEAD_DIM, anew_ref[0],
                       lambda c: bias_ref[0, :, c * SAMP_CHUNK:(c + 1) * SAMP_CHUNK],
                       bias_ref[0, :, past:past + 1])

    lf = lbuf[slot]
    sl = sl_ref[...]
    hi, mid, lo = _split3(lf)
    rev = _dot(hi, sl) + _dot(mid, sl) + _dot(lo, sl)
    tot = rev[:, 0:1] + lf[:, 0:1]
    run = jnp.zeros((N_HEADS, 1), F32)
    rp = [None] * n_pages
    for p in reversed(range(n_pages)):
        ps = slice(p * N_HEADS, (p + 1) * N_HEADS)
        rp[p] = rev[ps] + run
        run = run + tot[ps]
    nq = lfnew_ref[0]
    zpad = jnp.zeros((N_HEADS, SAMP_CHUNK), F32)

    def bias_b(c):
        rpc = jnp.concatenate(rp[c * pages_per_chunk:(c + 1) * pages_per_chunk], axis=1)
        return jnp.concatenate([rpc, zpad], axis=0) + nq

    for c in range(n_chunks):
        cs = slice(c * SAMP_CHUNK, (c + 1) * SAMP_CHUNK)
        bbf[:, cs] = bbuf[slot, :, cs].astype(bbf.dtype)
    out_b = attend(qb_ref[0], bbf, 2 * HEAD_DIM, bnew_ref[0], bias_b, jnp.zeros((2 * N_HEADS, 1), F32))
    head = lax.broadcasted_iota(jnp.int32, out_b.shape, 0)
    second_group = head >= N_HEADS // B_KV_HEADS
    ob_ref[0] = jnp.where(second_group, pltpu.roll(out_b, HEAD_DIM, 1), out_b)


def _samp_attn(page_table, qa16, qb16, bias, a_new, b_new, lf_new16, cache_a, cache_b, cache_lft):
    db, n_pages = page_table.shape
    past = n_pages * PAGE
    sl = jnp.tril(jnp.ones((LANES, LANES), F32), k=-1).astype(MXU_DTYPE)
    body = functools.partial(_samp_attn_body, n_pages=n_pages)
    row = lambda b, pt: (b, 0, 0)
    any_spec = pl.BlockSpec(memory_space=pl.ANY)
    return pl.pallas_call(
        body,
        grid_spec=pltpu.PrefetchScalarGridSpec(
            num_scalar_prefetch=1, grid=(db,),
            in_specs=[pl.BlockSpec((1,) + qa16.shape[1:], row), pl.BlockSpec((1,) + qb16.shape[1:], row),
                      pl.BlockSpec((1,) + bias.shape[1:], row), pl.BlockSpec((1,) + a_new.shape[1:], row),
                      pl.BlockSpec((1,) + b_new.shape[1:], row), pl.BlockSpec((1,) + lf_new16.shape[1:], row),
                      pl.BlockSpec(sl.shape, lambda b, pt: (0, 0)), any_spec, any_spec, any_spec],
            out_specs=[pl.BlockSpec((1, 2 * N_HEADS, HEAD_DIM), row),
                       pl.BlockSpec((1, 2 * N_HEADS, 2 * HEAD_DIM), row)],
            scratch_shapes=[pltpu.VMEM((2, 2 * HEAD_DIM, past), F32), pltpu.VMEM((2, 4 * HEAD_DIM, past), F32),
                            pltpu.VMEM((2, n_pages * N_HEADS, PAGE), F32),
                            pltpu.SemaphoreType.DMA((2,)), pltpu.SemaphoreType.DMA((2,)),
                            pltpu.SemaphoreType.DMA((2,)),
                            pltpu.VMEM((2 * HEAD_DIM, past), MXU_DTYPE), pltpu.VMEM((4 * HEAD_DIM, past), MXU_DTYPE)]),
        out_shape=[jax.ShapeDtypeStruct((db, 2 * N_HEADS, HEAD_DIM), F32),
                   jax.ShapeDtypeStruct((db, 2 * N_HEADS, 2 * HEAD_DIM), F32)],
        compiler_params=pltpu.CompilerParams(dimension_semantics=("arbitrary",), vmem_limit_bytes=VMEM_LIMIT),
        name="samp_attn",
    )(page_table.reshape(-1), qa16, qb16, bias, a_new, b_new, lf_new16, sl, cache_a, cache_b, cache_lft)


def _ffn_body(x_ref, o_ref, ge_ref, be_ref, wo_ref, g1_ref, b1_ref, wg_ref, wu_ref, wd_ref, g2_ref, b2_ref,
              y_ref, *, alpha, ff_chunk):
    h = _layer_norm(x_ref[...], ge_ref[...], be_ref[...])
    mixed = _dot(o_ref[...], wo_ref[...])
    h1 = _layer_norm(alpha * h + mixed, g1_ref[...], b1_ref[...])
    h1b = h1.astype(MXU_DTYPE)
    d_ff = wg_ref.shape[1]
    ffn = jnp.zeros(h1.shape, F32)
    for c in range(d_ff // ff_chunk):
        cs = slice(c * ff_chunk, (c + 1) * ff_chunk)
        gate = _dot(h1b, wg_ref[:, cs])
        up = _dot(h1b, wu_ref[:, cs])
        act = (gate * jax.nn.sigmoid(gate)) * up
        ffn = ffn + _dot(act.astype(MXU_DTYPE), wd_ref[cs, :])
    y_ref[...] = _layer_norm(alpha * h1 + ffn, g2_ref[...], b2_ref[...])


def _ffn_chunk(d_ff):
    for n in (11, 8, 6, 4, 2, 1):
        if d_ff % (n * LANES) == 0:
            return n * LANES
    raise ValueError(f"d_ff={d_ff} is not a multiple of {LANES}")


def _ffn_call(x2, o2, ln_e, w_o, ln1, w_gate, w_up, w_down, ln2, alpha, tm):
    m, d = x2.shape
    assert m % tm == 0
    d_ff = w_gate.shape[1]
    row = lambda i: (i, 0)
    const = lambda i: (0, 0)
    vec = pl.BlockSpec((1, d), const)
    resident = lambda shape: pl.BlockSpec(shape, const, pipeline_mode=pl.Buffered(1))
    body = functools.partial(_ffn_body, alpha=alpha, ff_chunk=_ffn_chunk(d_ff))
    return pl.pallas_call(
        body,
        grid=(m // tm,),
        in_specs=[pl.BlockSpec((tm, d), row), pl.BlockSpec((tm, o2.shape[1]), row), vec, vec,
                  resident(w_o.shape), vec, vec, resident(w_gate.shape), resident(w_up.shape),
                  resident(w_down.shape), vec, vec],
        out_specs=pl.BlockSpec((tm, d), row),
        out_shape=jax.ShapeDtypeStruct((m, d), F32),
        compiler_params=pltpu.CompilerParams(dimension_semantics=("arbitrary",), vmem_limit_bytes=VMEM_LIMIT),
        name="ffn",
    )(x2, o2, ln_e[0].reshape(1, d), ln_e[1].reshape(1, d), w_o, ln1[0].reshape(1, d), ln1[1].reshape(1, d),
      w_gate, w_up, w_down, ln2[0].reshape(1, d), ln2[1].reshape(1, d))


def _row_tile(n, cap):
    t = min(n, cap)
    while n % t:
        t //= 2
    return t


def kernel(x_prompt, x_sample, cache_a_kv, cache_idx_k, cache_b_kv, cache_b_logf, page_table, meta,
           ln_emb_g, ln_emb_b, w_in, b_forget, w_o, ln1_g, ln1_b, w_gate, w_up, w_down, ln2_g, ln2_b):
    depth = w_in.shape[0]
    assert depth == 1, "single-layer trunk only"
    bsz, seq, d = x_prompt.shape
    db, ds, _ = x_sample.shape
    assert ds == 1, "one new token per decode sequence"
    n_meta = meta.shape[0]
    n_pages = page_table.shape[1]
    past = n_pages * PAGE
    alpha = (2.0 * depth) ** 0.25
    topk_p = min(TOPK_MAX, seq // 4)
    topk_s = min(TOPK_MAX, (past + ds) // 4)

    sizes = (512, 64, 64, 512, 64, 8, 512, 128, 128, 8)
    offs = [0]
    for sz in sizes:
        offs.append(offs[-1] + sz)
    w0 = w_in[0]
    part = lambda i: w0[:, offs[i]:offs[i + 1]]
    qa_w, ka_w, va_w, qi_w, ki_w, wi_w, qb_w, kb_w, vb_w, fl_w = [part(i) for i in range(10)]
    pad_w = jnp.zeros((d, _C_KVB - _C_KIW - HEAD_DIM - 2 * N_HEADS), w0.dtype)
    w_cat = jnp.concatenate([qa_w, qi_w, qb_w, ka_w, va_w, ki_w, wi_w, fl_w, pad_w, kb_w, vb_w],
                            axis=1).astype(MXU_DTYPE)
    zero_c = jnp.zeros((1, N_HEADS), F32)
    proj = functools.partial(_proj_call, w_cat=w_cat, ln_g=ln_emb_g, ln_b=ln_emb_b, b_forget=b_forget[0])

    (_, wc_m, kidx_m, kva_m, kvb_m, akv_m, idxk_m, bkv_m, lf_m, kext_m) = proj(
        meta[None].astype(x_prompt.dtype), jnp.arange(n_meta), zero_c, tm=n_meta)
    c_meta = wc_m[0, :, N_HEADS:]
    (q_t, wc, kidx, kva, kvb, akv, idxk, bkv, lf, kext) = proj(
        x_prompt, n_meta + jnp.arange(seq), c_meta[n_meta - 1:n_meta], tm=_row_tile(seq, 512), q_token_minor=True)
    (qcat_s, wc_s, _, _, _, akv_s, idxk_s, bkv_s, lf_s, _) = proj(
        x_sample.reshape(1, db, d), jnp.full((db,), past, jnp.int32), zero_c, tm=_row_tile(db, 512))

    pad_rows = lambda t: jnp.pad(t[0], ((0, LANES - n_meta), (0, 0)))
    t_last = lambda t: jnp.swapaxes(t, -1, -2)
    vat, vbt = t_last(kva[:, :, HEAD_DIM:]), t_last(kvb[:, :, 2 * HEAD_DIM:])
    vat_m, vbt_m = t_last(pad_rows(kva_m)[:, HEAD_DIM:]), t_last(pad_rows(kvb_m)[:, 2 * HEAD_DIM:])
    o_t = _prompt_attn(q_t, t_last(wc), kidx, kva, kvb, vat, vbt, kext,
                       pad_rows(kidx_m), pad_rows(kva_m), pad_rows(kvb_m), vat_m, vbt_m, pad_rows(kext_m),
                       n_meta, topk_p)
    o_prompt = t_last(o_t)

    qs = qcat_s[0].astype(F32).reshape(db, 3, N_HEADS, HEAD_DIM)
    pad_heads = lambda t: jnp.pad(t, ((0, 0), (0, N_HEADS), (0, 0)))
    qa16 = pad_heads(qs[:, 0]).astype(MXU_DTYPE)
    qi16 = pad_heads(qs[:, 1]).astype(MXU_DTYPE)
    half = N_HEADS // B_KV_HEADS
    head_group = jnp.arange(N_HEADS)[:, None] // half
    lane_group = jnp.arange(2 * HEAD_DIM)[None, :] // HEAD_DIM
    qb16 = pad_heads(jnp.where(head_group == lane_group, jnp.concatenate([qs[:, 2], qs[:, 2]], axis=2),
                               0.0)).astype(MXU_DTYPE)
    wi16 = pad_heads(wc_s[0, :, :N_HEADS, None])
    lfn16 = pad_heads(lf_s[0][:, :, None])
    feat_major = lambda c: jnp.moveaxis(c[0], 1, -1).reshape(c.shape[1], -1, PAGE)
    scores = _samp_scores(page_table, qi16, wi16, idxk_s[0][:, None, :], feat_major(cache_idx_k))
    n_t = n_pages + 1
    scores_tt = jnp.transpose(scores.reshape(db, n_t, PAGE), (1, 2, 0))
    selb_tt = _samp_select(scores_tt, n_pages, topk_s)
    bias = jnp.transpose(selb_tt, (2, 0, 1)).reshape(db, 1, n_t * PAGE)
    oa_s, ob_s = _samp_attn(page_table, qa16, qb16, bias, akv_s[0][:, None, :], bkv_s[0][:, None, :], lfn16,
                            feat_major(cache_a_kv), feat_major(cache_b_kv), feat_major(cache_b_logf))
    heads_flat = lambda t: t[:, :N_HEADS, :HEAD_DIM].reshape(db, N_HEADS * HEAD_DIM)
    o_sample = jnp.concatenate([heads_flat(oa_s), heads_flat(ob_s)], axis=1).astype(MXU_DTYPE)

    cast = lambda w: w[0].astype(MXU_DTYPE)
    ffn = functools.partial(_ffn_call, ln_e=(ln_emb_g, ln_emb_b), w_o=cast(w_o), ln1=(ln1_g[0], ln1_b[0]),
                            w_gate=cast(w_gate), w_up=cast(w_up), w_down=cast(w_down), ln2=(ln2_g[0], ln2_b[0]),
                            alpha=alpha)
    y_prompt = ffn(x_prompt.reshape(bsz * seq, d), o_prompt.reshape(bsz * seq, -1),
                   tm=_row_tile(bsz * seq, 512)).reshape(bsz, seq, d)
    y_sample = ffn(x_sample.reshape(db, d), o_sample, tm=_row_tile(db, 512)).reshape(db, ds, d)

    def with_meta(m_part, x_part, tail):
        m_b = jnp.broadcast_to(m_part, (bsz,) + m_part.shape[1:])
        return jnp.concatenate([m_b, x_part], axis=1).reshape((1, bsz, n_meta + seq) + tail)

    return (y_prompt, y_sample,
            with_meta(akv_m, akv, (2, 1, HEAD_DIM)),
            with_meta(idxk_m, idxk, (HEAD_DIM,)),
            with_meta(bkv_m, bkv, (2, B_KV_HEADS, HEAD_DIM)),
            with_meta(lf_m, lf, (N_HEADS,)),
            akv_s[0].reshape(1, db, ds, 2, 1, HEAD_DIM),
            idxk_s[0].reshape(1, db, ds, HEAD_DIM),
            bkv_s[0].reshape(1, db, ds, 2, B_KV_HEADS, HEAD_DIM),
            lf_s[0].reshape(1, db, ds, N_HEADS))
```

```python
import functools

import jax
import jax.numpy as jnp
from jax import lax
from jax.experimental import pallas as pl
from jax.experimental.pallas import tpu as pltpu

HEAD_DIM = 64
N_HEADS = 8
B_KV_HEADS = 2
ROPE_HALF = 8
ROPE_THETA = 500000.0
TOPK_MAX = 256
PAGE = 128
LN_EPS = 1e-5
Q_BLOCK = 128
KEY_TILE = 256
LANES = 128
SUBLANES = 8
RADIX_STEPS = 4
ONES_ROWS = 16
C_ONES_AT = 32
INT_MIN = -2 ** 31
NEG_INF = float("-inf")
VMEM_LIMIT = 56 * 1024 * 1024

F32 = jnp.float32
BF16 = jnp.bfloat16
MXU_DTYPE = BF16

_NT = (((1,), (1,)), ((), ()))


def _dot(a, b):
    return jnp.dot(a, b, preferred_element_type=F32)


def _dot_nt(a, b):
    return lax.dot_general(a, b, _NT, preferred_element_type=F32)


def _layer_norm(x, g, b):
    mu = jnp.mean(x, axis=-1, keepdims=True)
    xc = x - mu
    var = jnp.mean(xc * xc, axis=-1, keepdims=True)
    return xc * lax.rsqrt(var + LN_EPS) * g + b


def _split3(x):
    hi = x.astype(MXU_DTYPE)
    r1 = x - hi.astype(F32)
    mid = r1.astype(MXU_DTYPE)
    lo = (r1 - mid.astype(F32)).astype(MXU_DTYPE)
    return hi, mid, lo


_C_QA, _C_QI, _C_QB, _C_KVA, _C_KIW, _C_KVB, _C_END = 0, 512, 1024, 1536, 1664, 1792, 2048


def _rope(v, cos, sa, sb):
    return v * cos + pltpu.roll(v, LANES - ROPE_HALF, 1) * sa + pltpu.roll(v, ROPE_HALF, 1) * sb


def _proj_body(x_ref, g_ref, b_ref, w_ref, bf_ref, cos1_ref, sa1_ref, sb1_ref, cos2_ref, sa2_ref, sb2_ref,
               tri_ref, c0_ref,
               q_ref, wc_ref, kidx_ref, kva_ref, kvb_ref, akv_ref, idxk_ref, bkv_ref, lf_ref, kext_ref,
               carry_ref, *, q_token_minor):
    j = pl.program_id(1)
    tm = x_ref.shape[1]
    h = _layer_norm(x_ref[0], g_ref[...], b_ref[...])
    proj = _dot(h.astype(MXU_DTYPE), w_ref[...])

    cos1, sa1, sb1 = cos1_ref[...], sa1_ref[...], sb1_ref[...]
    scale = HEAD_DIM ** -0.5
    for g in range(_C_KVA // LANES):
        gs = slice(g * LANES, (g + 1) * LANES)
        seg = proj[:, gs]
        seg = (_rope(seg, cos1, sa1, sb1) if g < _C_QB // LANES else seg) * scale
        if q_token_minor:
            q_ref[0, gs, :] = seg.T.astype(q_ref.dtype)
        else:
            q_ref[0, :, gs] = seg.astype(q_ref.dtype)

    cos2, sa2, sb2 = cos2_ref[...], sa2_ref[...], sb2_ref[...]
    kva = _rope(proj[:, _C_KVA:_C_KIW], cos2, sa2, sb2)
    akv_ref[0] = kva
    kva_ref[0] = kva.astype(kva_ref.dtype)
    kiw = _rope(proj[:, _C_KIW:_C_KVB], cos2, sa2, sb2)
    idxk_ref[0] = kiw[:, :HEAD_DIM]
    kidx_ref[0] = kiw[:, :HEAD_DIM].astype(kidx_ref.dtype)
    wi = kiw[:, HEAD_DIM:HEAD_DIM + N_HEADS] * (N_HEADS ** -0.5)
    z = kiw[:, HEAD_DIM + N_HEADS:HEAD_DIM + 2 * N_HEADS] + bf_ref[...]
    lf = jnp.minimum(z, 0.0) - jnp.log1p(jnp.exp(-jnp.abs(z)))
    lf_ref[0] = lf
    kvb = proj[:, _C_KVB:_C_END]
    bkv_ref[0] = kvb
    kvb_ref[0] = kvb.astype(kvb_ref.dtype)

    @pl.when(j == 0)
    def _():
        carry_ref[...] = c0_ref[...]

    tri = tri_ref[...]
    hi, mid, lo = _split3(lf)
    cs = _dot(tri, hi) + _dot(tri, mid) + _dot(tri, lo) + carry_ref[...]
    carry_ref[...] = cs[tm - 1:tm, :]
    wc_ref[0] = jnp.concatenate([wi, cs], axis=1)
    pieces = [-p.astype(F32) for p in _split3(cs)]
    lane = lax.broadcasted_iota(jnp.int32, (tm, LANES - C_ONES_AT), 1)
    ones_at = jnp.where(lane < 3, 1.0, 0.0)
    kext_ref[0] = jnp.concatenate(pieces + [jnp.zeros((tm, C_ONES_AT - 3 * N_HEADS), F32), ones_at],
                                  axis=1).astype(kext_ref.dtype)


def _rope_tables(pos):
    inv = ROPE_THETA ** (-jnp.arange(ROPE_HALF, dtype=F32) / ROPE_HALF)
    ang = pos.astype(F32)[:, None] * inv
    cos, sin = jnp.cos(ang), jnp.sin(ang)
    ones = jnp.ones((pos.shape[0], HEAD_DIM - 2 * ROPE_HALF), F32)
    zeros = jnp.zeros_like(ones)
    zh = jnp.zeros_like(sin)
    cos_h = jnp.concatenate([cos, cos, ones], 1)
    sa_h = jnp.concatenate([-sin, zh, zeros], 1)
    sb_h = jnp.concatenate([zh, sin, zeros], 1)
    one_h, zero_h = jnp.ones_like(cos_h), jnp.zeros_like(cos_h)
    t1 = [jnp.concatenate([t, t], 1) for t in (cos_h, sa_h, sb_h)]
    t2 = [jnp.concatenate([cos_h, one_h], 1), jnp.concatenate([sa_h, zero_h], 1),
          jnp.concatenate([sb_h, zero_h], 1)]
    return t1 + t2


def _proj_call(x3, pos, c0, w_cat, ln_g, ln_b, b_forget, tm, q_token_minor=False):
    nb, s, d = x3.shape
    assert s % tm == 0
    tables = _rope_tables(pos)
    tri = jnp.tril(jnp.ones((tm, tm), F32)).astype(MXU_DTYPE)
    row = lambda b, j: (b, j, 0)
    const2 = lambda b, j: (0, 0)
    tab_spec = pl.BlockSpec((tm, LANES), lambda b, j: (j, 0))
    widths = (3 * 512, 2 * N_HEADS, HEAD_DIM, 2 * HEAD_DIM, 4 * HEAD_DIM, 2 * HEAD_DIM, HEAD_DIM, 4 * HEAD_DIM,
              N_HEADS, LANES)
    dtypes = (MXU_DTYPE, F32, MXU_DTYPE, MXU_DTYPE, MXU_DTYPE, F32, F32, F32, F32, MXU_DTYPE)
    out_specs = [pl.BlockSpec((1, tm, w), row) for w in widths]
    out_shape = [jax.ShapeDtypeStruct((nb, s, w), dt) for w, dt in zip(widths, dtypes)]
    if q_token_minor:
        out_specs[0] = pl.BlockSpec((1, widths[0], tm), lambda b, j: (b, 0, j))
        out_shape[0] = jax.ShapeDtypeStruct((nb, widths[0], s), dtypes[0])
    return pl.pallas_call(
        functools.partial(_proj_body, q_token_minor=q_token_minor),
        grid=(nb, s // tm),
        in_specs=[pl.BlockSpec((1, tm, d), row),
                  pl.BlockSpec((1, d), const2), pl.BlockSpec((1, d), const2),
                  pl.BlockSpec(w_cat.shape, const2),
                  pl.BlockSpec((1, N_HEADS), const2)] + [tab_spec] * 6 +
                 [pl.BlockSpec((tm, tm), const2), pl.BlockSpec((1, N_HEADS), const2)],
        out_specs=out_specs,
        out_shape=out_shape,
        scratch_shapes=[pltpu.VMEM((1, N_HEADS), F32)],
        compiler_params=pltpu.CompilerParams(dimension_semantics=("arbitrary", "arbitrary"),
                                             vmem_limit_bytes=VMEM_LIMIT),
        name="proj",
    )(x3, ln_g.reshape(1, d), ln_b.reshape(1, d), w_cat, b_forget.reshape(1, N_HEADS), *tables, tri, c0)


def _ordered_int_to_float(key):
    return pltpu.bitcast(key ^ ((key >> 31) & jnp.int32(0x7FFFFFFF)), F32)


def _select_topk(sc_ref, selb_ref, lt_ref, n_tiles, k):
    rows = sc_ref.shape[1]
    chains = 4
    part = chains * SUBLANES
    kf = float(k)

    def count(pred):
        def one(t, acc):
            hit = jnp.where(pred(sc_ref[t]), 1.0, 0.0)
            return acc + jnp.sum(hit.reshape(rows // part, part, LANES), axis=0)

        def pair(i, acc):
            return one(2 * i + 1, one(2 * i, acc))
        acc = lax.fori_loop(0, n_tiles // 2, pair, jnp.zeros((part, LANES), F32))
        acc = lax.cond(n_tiles % 2 == 1, lambda a: one(n_tiles - 1, a), lambda a: a, acc)
        return jnp.sum(acc, axis=0, keepdims=True)

    def n_unsettled(cnt):
        return jnp.sum(jnp.where(cnt != kf, 1.0, 0.0))

    n_adm = count(lambda x: x > NEG_INF)
    small = n_adm <= kf
    cur0 = jnp.full((1, LANES), INT_MIN, jnp.int32)
    cnt0 = jnp.where(small, kf, n_adm)

    def cond(st):
        it, _, _, bad = st
        return jnp.logical_and(it < 32 // RADIX_STEPS, bad > 0.0)

    def body(st):
        it, cur, cnt, _ = st
        for step in range(RADIX_STEPS):
            bit = 31 - (it * RADIX_STEPS + step)
            cand = cur + jnp.left_shift(jnp.int32(1), bit)
            thr = _ordered_int_to_float(cand)
            c = count(lambda x: x >= thr)
            take = c >= kf
            cur = jnp.where(take, cand, cur)
            cnt = jnp.where(take, c, cnt)
        return it + 1, cur, cnt, n_unsettled(cnt)

    _, cur, cnt, _ = lax.while_loop(cond, body, (jnp.int32(0), cur0, cnt0, n_unsettled(cnt0)))
    has_ties = jnp.sum(jnp.where(cnt > kf, 1.0, 0.0)) > 0.0
    thr = _ordered_int_to_float(jnp.where(small, INT_MIN + (1 << 23), cur))

    @pl.when(jnp.logical_not(has_ties))
    def _():
        def body(t, carry):
            selb_ref[t] = jnp.where(sc_ref[t] >= thr, 0.0, NEG_INF)
            return carry
        lax.fori_loop(0, n_tiles, body, 0)

    @pl.when(has_ties)
    def _():
        need = kf - count(lambda x: x > thr)
        lt = lt_ref[...]

        def body(t, seen):
            st = sc_ref[t]
            eqf = jnp.where(st == thr, 1.0, 0.0)
            rank = _dot(lt, eqf.astype(MXU_DTYPE)) + seen
            keep_tie = jnp.where(rank <= need, eqf, 0.0)
            selb_ref[t] = jnp.where(st > thr, 0.0, jnp.where(keep_tie > 0.0, 0.0, NEG_INF))
            return seen + jnp.sum(eqf, axis=0, keepdims=True)
        lax.fori_loop(0, n_tiles, body, jnp.zeros((1, LANES), F32))


def _prompt_attn_body(qt_ref, wct_ref, kidx_ref, kva_ref, kvb_ref, vat_ref, vbt_ref, kext_ref,
                      kidxm_ref, kvam_ref, kvbm_ref, vatm_ref, vbtm_ref, kextm_ref, lt_ref, sel_ref,
                      o_ref,
                      kidx_c, kva_c, kb_c, vat_c, vbt_c, keys_ref, cmask_ref, selb_ref,
                      lga_ref, lgb_ref, acca_ref, accb_ref, *, n_meta, topk):
    j = pl.program_id(1)
    s = kidx_ref.shape[1]
    kp = kidx_c.shape[0]
    n_u = keys_ref.shape[0]
    cols = N_HEADS * Q_BLOCK
    groups = KEY_TILE // SUBLANES

    @pl.when(j == 0)
    def _():
        tail = kp - LANES - s
        for dst, lane0, m_src, x_src, width in ((kidx_c, 0, kidxm_ref, kidx_ref, HEAD_DIM),
                                                (kva_c, 0, kvam_ref, kva_ref, 2 * HEAD_DIM),
                                                (kb_c, 0, kvbm_ref, kvb_ref, 2 * HEAD_DIM),
                                                (kb_c, 2 * HEAD_DIM, kextm_ref, kext_ref, LANES)):
            lanes = slice(lane0, lane0 + width)
            dst[0:LANES, lanes] = m_src[:, 0:width]
            dst[LANES:LANES + s, lanes] = x_src[0, :, 0:width]
            if tail:
                dst[LANES + s:, lanes] = jnp.zeros((tail, width), dst.dtype)
        ones = jnp.ones((ONES_ROWS, kp), vat_c.dtype)
        for dst, m_src, x_src in ((vat_c, vatm_ref, vat_ref), (vbt_c, vbtm_ref, vbt_ref)):
            full = jnp.concatenate([m_src[...], x_src[0], jnp.zeros((m_src.shape[0], tail), dst.dtype)], axis=1)
            parts = []
            for g in range(m_src.shape[0] // HEAD_DIM):
                parts += [full[g * HEAD_DIM:(g + 1) * HEAD_DIM], ones]
            full = jnp.concatenate(parts, axis=0)
            for u in range(n_u):
                dst[u] = full[:, u * KEY_TILE:(u + 1) * KEY_TILE]

    qt = qt_ref[0]

    def heads_on_lanes(off):
        return jnp.concatenate([qt[off + h * HEAD_DIM:off + (h + 1) * HEAD_DIM, :] for h in range(N_HEADS)], axis=1)

    qa_t, qi_t, qb_t = heads_on_lanes(_C_QA), heads_on_lanes(_C_QI), heads_on_lanes(_C_QB)
    z_all = jnp.zeros((HEAD_DIM, cols), qt.dtype)
    z_half = jnp.zeros((HEAD_DIM, cols // 2), qt.dtype)
    qa_pad = jnp.concatenate([qa_t, z_all], axis=0)
    qb_bd = jnp.concatenate([jnp.concatenate([qb_t[:, :cols // 2], z_half], axis=1),
                             jnp.concatenate([z_half, qb_t[:, cols // 2:]], axis=1)], axis=0)
    wct = wct_ref[0]
    cq = jnp.concatenate([wct[N_HEADS + h:N_HEADS + h + 1, :] for h in range(N_HEADS)], axis=1)
    row = lax.broadcasted_iota(jnp.int32, (ONES_ROWS, cols), 0)
    cq_rows = jnp.zeros((ONES_ROWS, cols), F32)
    for i, piece in enumerate(_split3(cq)):
        cq_rows = jnp.where(row == i, piece.astype(F32), cq_rows)
    qb_ext = jnp.concatenate([qb_bd, sel_ref[...], cq_rows.astype(qt.dtype),
                              jnp.zeros((LANES - C_ONES_AT - ONES_ROWS, cols), qt.dtype)], axis=0)

    per = KEY_TILE // LANES
    n_tiles = jnp.right_shift(j + 1 + per, per.bit_length() - 1)

    def tile_rows(u):
        return pl.ds(pl.multiple_of(u * KEY_TILE, KEY_TILE), KEY_TILE)

    def pass1(q_t, k_tile, bias, lg_ref):
        def step(u, m):
            lg = _dot(k_tile(u), q_t)
            ms = []
            for h in range(N_HEADS):
                hs = slice(h * Q_BLOCK, (h + 1) * Q_BLOCK)
                lgh = lg[:, hs] + bias(u, h)
                lg_ref[u, :, hs] = lgh
                ms.append(jnp.max(lgh.reshape(groups, SUBLANES, Q_BLOCK), axis=0))
            return jnp.maximum(m, jnp.concatenate(ms, axis=1))
        return step

    def pass2(lg_ref, m_row, pv):
        def step(u):
            p = jnp.exp(lg_ref[u] - m_row)
            pv(u, p.astype(MXU_DTYPE))
        return step

    def col_max(m):
        return jnp.max(m, axis=0, keepdims=True)

    m_init = jnp.full((SUBLANES, cols), NEG_INF, F32)
    v_rows = HEAD_DIM + ONES_ROWS

    def pv_a(u, p):
        acca_ref[...] += _dot(vat_c[u], p)

    def pv_b(u, p):
        vt = vbt_c[u]
        accb_ref[:, 0:cols // 2] += _dot(vt[0:v_rows], p[:, 0:cols // 2])
        accb_ref[:, cols // 2:] += _dot(vt[v_rows:], p[:, cols // 2:])

    def normalised(acc_ref):
        return acc_ref[0:HEAD_DIM, :] / acc_ref[HEAD_DIM:HEAD_DIM + 1, :]

    a_pass1 = pass1(qa_pad, lambda u: kva_c[tile_rows(u), :], lambda u, h: selb_ref[u], lga_ref)
    b_pass1 = pass1(qb_ext, lambda u: kb_c[tile_rows(u), :], lambda u, h: cmask_ref[u], lgb_ref)

    def idx_step(u):
        d = _dot(kidx_c[tile_rows(u), :], qi_t)
        sc = jnp.zeros((KEY_TILE, Q_BLOCK), F32)
        for h in range(N_HEADS):
            sc = sc + wct[h:h + 1, :] * jnp.maximum(d[:, h * Q_BLOCK:(h + 1) * Q_BLOCK], 0.0)
        key_i = u * KEY_TILE + lax.broadcasted_iota(jnp.int32, (KEY_TILE, Q_BLOCK), 0)
        qry_i = lax.broadcasted_iota(jnp.int32, (KEY_TILE, Q_BLOCK), 1)
        valid = jnp.logical_or(key_i < n_meta,
                               jnp.logical_and(key_i >= LANES, key_i - LANES <= j * Q_BLOCK + qry_i))
        keys_ref[u] = jnp.where(valid, sc, NEG_INF)
        cmask_ref[u] = jnp.where(valid, 0.0, NEG_INF)

    def over_tiles(step, carry):
        def pair(i, c):
            return step(2 * i + 1, step(2 * i, c))
        carry = lax.fori_loop(0, jnp.right_shift(n_tiles, 1), pair, carry)
        return lax.cond(jnp.bitwise_and(n_tiles, 1) == 1, lambda c: step(n_tiles - 1, c), lambda c: c, carry)

    def idx_and_b1(u, m_b):
        idx_step(u)
        return b_pass1(u, m_b)
    m_b = col_max(over_tiles(idx_and_b1, m_init))

    _select_topk(keys_ref, selb_ref, lt_ref, n_tiles, topk)

    acca_ref[...] = jnp.zeros(acca_ref.shape, F32)
    accb_ref[...] = jnp.zeros(accb_ref.shape, F32)
    b_pass2 = pass2(lgb_ref, m_b, pv_b)

    def a1_and_b2(u, m_a):
        b_pass2(u)
        return a_pass1(u, m_a)
    m_a = col_max(over_tiles(a1_and_b2, m_init))
    a_pass2 = pass2(lga_ref, m_a, pv_a)

    def a2(u, carry):
        a_pass2(u)
        return carry
    over_tiles(a2, 0)

    out_a = normalised(acca_ref)
    for h in range(N_HEADS):
        o_ref[0, h * HEAD_DIM:(h + 1) * HEAD_DIM, :] = out_a[:, h * Q_BLOCK:(h + 1) * Q_BLOCK].astype(o_ref.dtype)
    out_b = normalised(accb_ref)
    for h in range(N_HEADS):
        o_ref[0, (N_HEADS + h) * HEAD_DIM:(N_HEADS + h + 1) * HEAD_DIM, :] = (
            out_b[:, h * Q_BLOCK:(h + 1) * Q_BLOCK].astype(o_ref.dtype))


def _prompt_attn(qt, wct, kidx, kva, kvb, vat, vbt, kext, kidx_m, kva_m, kvb_m, vat_m, vbt_m, kext_m, n_meta,
                 topk):
    nb, _, s = qt.shape
    assert s % Q_BLOCK == 0
    nq = s // Q_BLOCK
    per = KEY_TILE // LANES
    n_u = (nq + per) // per
    kp = n_u * KEY_TILE
    cols = N_HEADS * Q_BLOCK
    v_rows = HEAD_DIM + ONES_ROWS
    lt = jnp.tril(jnp.ones((KEY_TILE, KEY_TILE), F32)).astype(MXU_DTYPE)
    sel_row, sel_col = jnp.arange(C_ONES_AT)[:, None], jnp.arange(cols)[None, :]
    sel = jnp.where((sel_row < 3 * N_HEADS) & (sel_row % N_HEADS == sel_col // Q_BLOCK), 1.0, 0.0).astype(MXU_DTYPE)
    qblk = lambda b, j: (b, 0, j)
    per_b = lambda b, j: (b, 0, 0)
    const2 = lambda b, j: (0, 0)
    whole = lambda a: pl.BlockSpec(a.shape, const2)
    body = functools.partial(_prompt_attn_body, n_meta=n_meta, topk=topk)
    return pl.pallas_call(
        body,
        grid=(nb, nq),
        in_specs=[pl.BlockSpec((1, qt.shape[1], Q_BLOCK), qblk),
                  pl.BlockSpec((1, wct.shape[1], Q_BLOCK), qblk)] +
                 [pl.BlockSpec((1,) + a.shape[1:], per_b) for a in (kidx, kva, kvb, vat, vbt, kext)] +
                 [whole(a) for a in (kidx_m, kva_m, kvb_m, vat_m, vbt_m, kext_m, lt, sel)],
        out_specs=pl.BlockSpec((1, 2 * N_HEADS * HEAD_DIM, Q_BLOCK), qblk),
        out_shape=jax.ShapeDtypeStruct((nb, 2 * N_HEADS * HEAD_DIM, s), MXU_DTYPE),
        scratch_shapes=[pltpu.VMEM((kp, HEAD_DIM), MXU_DTYPE),
                        pltpu.VMEM((kp, 2 * HEAD_DIM), MXU_DTYPE),
                        pltpu.VMEM((kp, 2 * HEAD_DIM + LANES), MXU_DTYPE),
                        pltpu.VMEM((n_u, v_rows, KEY_TILE), MXU_DTYPE),
                        pltpu.VMEM((n_u, B_KV_HEADS * v_rows, KEY_TILE), MXU_DTYPE),
                        pltpu.VMEM((n_u, KEY_TILE, Q_BLOCK), F32),
                        pltpu.VMEM((n_u, KEY_TILE, Q_BLOCK), F32),
                        pltpu.VMEM((n_u, KEY_TILE, Q_BLOCK), F32),
                        pltpu.VMEM((n_u, KEY_TILE, cols), F32),
                        pltpu.VMEM((n_u, KEY_TILE, cols), F32),
                        pltpu.VMEM((v_rows, cols), F32),
                        pltpu.VMEM((v_rows, cols), F32)],
        compiler_params=pltpu.CompilerParams(dimension_semantics=("arbitrary", "arbitrary"),
                                             vmem_limit_bytes=VMEM_LIMIT),
        name="prompt_attn",
    )(qt, wct, kidx, kva, kvb, vat, vbt, kext, kidx_m, kva_m, kvb_m, vat_m, vbt_m, kext_m, lt, sel)


def _lane_window(buf, slot, p):
    return buf.at[slot, :, pl.ds(pl.multiple_of(p * PAGE, PAGE), PAGE)]


def _row_window(buf, slot, p):
    rows = N_HEADS
    return buf.at[slot, pl.ds(pl.multiple_of(p * rows, rows), rows), :]


def _page_copy(stream, page, slot, p):
    src_hbm, buf, window, sem = stream
    return pltpu.make_async_copy(src_hbm.at[page], window(buf, slot, p), sem.at[slot])


def _start_pages(pt_ref, sample, slot, n_pages, streams):
    assert n_pages % PAGE_UNROLL == 0

    def trip(i, carry):
        for k in range(PAGE_UNROLL):
            p = i * PAGE_UNROLL + k
            page = pt_ref[sample * n_pages + p]
            for stream in streams:
                _page_copy(stream, page, slot, p).start(priority=k % 2)
        return carry
    lax.fori_loop(0, n_pages // PAGE_UNROLL, trip, 0)


def _wait_pages(slot, n_pages, streams):
    def one(p, carry):
        for stream in streams:
            _page_copy(stream, 0, slot, p).wait()
        return carry
    lax.fori_loop(0, n_pages, one, 0, unroll=PAGE_UNROLL)


def _prefetch_schedule(pt_ref, n_pages, streams):
    b = pl.program_id(0)
    nb = pl.num_programs(0)
    slot = lax.rem(b, 2)

    @pl.when(b == 0)
    def _():
        _start_pages(pt_ref, 0, 0, n_pages, streams)

    @pl.when(b + 1 < nb)
    def _():
        _start_pages(pt_ref, b + 1, 1 - slot, n_pages, streams)

    _wait_pages(slot, n_pages, streams)
    return slot


SAMP_CHUNK = 1024
PAGE_UNROLL = 8


def _samp_scores_body(pt_ref, qi_ref, wi_ref, knew_ref, cache_hbm, out_ref, buf, sem, *, n_pages):
    slot = _prefetch_schedule(pt_ref, n_pages, [(cache_hbm, buf, _lane_window, sem)])
    past = n_pages * PAGE
    qi = qi_ref[0]
    w = wi_ref[0]
    for c in range(past // SAMP_CHUNK):
        cs = slice(c * SAMP_CHUNK, (c + 1) * SAMP_CHUNK)
        kt = buf[slot, :, cs].astype(MXU_DTYPE)
        d = _dot(qi, kt)
        out_ref[0, :, cs] = jnp.sum(w * jnp.maximum(d, 0.0), axis=0, keepdims=True)
    knew = knew_ref[0].astype(MXU_DTYPE).astype(F32)
    d_new = jnp.sum(qi.astype(F32) * knew, axis=1, keepdims=True)
    s_new = jnp.sum(w * jnp.maximum(d_new, 0.0), axis=0, keepdims=True)
    out_ref[0, :, past:past + LANES] = jnp.broadcast_to(s_new, (1, LANES))


def _samp_scores(page_table, qi16, wi16, k_new, cache_idx_k):
    db, n_pages = page_table.shape
    past = n_pages * PAGE
    assert past % SAMP_CHUNK == 0
    body = functools.partial(_samp_scores_body, n_pages=n_pages)
    row = lambda b, pt: (b, 0, 0)
    return pl.pallas_call(
        body,
        grid_spec=pltpu.PrefetchScalarGridSpec(
            num_scalar_prefetch=1, grid=(db,),
            in_specs=[pl.BlockSpec((1,) + qi16.shape[1:], row), pl.BlockSpec((1,) + wi16.shape[1:], row),
                      pl.BlockSpec((1, 1, HEAD_DIM), row), pl.BlockSpec(memory_space=pl.ANY)],
            out_specs=pl.BlockSpec((1, 1, past + LANES), row),
            scratch_shapes=[pltpu.VMEM((2, HEAD_DIM, past), F32), pltpu.SemaphoreType.DMA((2,))]),
        out_shape=jax.ShapeDtypeStruct((db, 1, past + LANES), F32),
        compiler_params=pltpu.CompilerParams(dimension_semantics=("arbitrary",), vmem_limit_bytes=VMEM_LIMIT),
        name="samp_scores",
    )(page_table.reshape(-1), qi16, wi16, k_new, cache_idx_k)


def _samp_select_body(sc_ref, lt_ref, selb_ref, keys_ref, *, n_pages, topk):
    n_t = keys_ref.shape[0]
    row = lax.broadcasted_iota(jnp.int32, (PAGE, LANES), 0)
    for t in range(n_t):
        key = sc_ref[t]
        if t == n_pages:
            key = jnp.where(row == 0, key, NEG_INF)
        keys_ref[t] = key
    _select_topk(keys_ref, selb_ref, lt_ref, n_t, topk)


def _samp_select(scores_tt, n_pages, topk):
    n_t, _, db = scores_tt.shape
    assert db % LANES == 0 and n_t == n_pages + 1
    lt = jnp.tril(jnp.ones((PAGE, PAGE), F32)).astype(MXU_DTYPE)
    body = functools.partial(_samp_select_body, n_pages=n_pages, topk=topk)
    blk = pl.BlockSpec((n_t, PAGE, LANES), lambda r: (0, 0, r))
    return pl.pallas_call(
        body,
        grid=(db // LANES,),
        in_specs=[blk, pl.BlockSpec(lt.shape, lambda r: (0, 0))],
        out_specs=blk,
        out_shape=jax.ShapeDtypeStruct(scores_tt.shape, F32),
        scratch_shapes=[pltpu.VMEM((n_t, PAGE, LANES), F32)],
        compiler_params=pltpu.CompilerParams(dimension_semantics=("arbitrary",), vmem_limit_bytes=VMEM_LIMIT),
        name="samp_select",
    )(scores_tt, lt)


def _samp_attn_body(pt_ref, qa_ref, qb_ref, bias_ref, anew_ref, bnew_ref, lfnew_ref, sl_ref,
                    akv_hbm, bkv_hbm, lft_hbm, oa_ref, ob_ref,
                    abuf, bbuf, lbuf, sem_a, sem_b, sem_l, abf, bbf, *, n_pages):
    slot = _prefetch_schedule(pt_ref, n_pages, [(akv_hbm, abuf, _lane_window, sem_a),
                                                (bkv_hbm, bbuf, _lane_window, sem_b),
                                                (lft_hbm, lbuf, _row_window, sem_l)])
    past = n_pages * PAGE
    n_chunks = past // SAMP_CHUNK
    pages_per_chunk = SAMP_CHUNK // PAGE

    def attend(q16, kv_bf, n_k, new_row, bias_chunk, bias_new):
        k_rows, v_rows = slice(0, n_k), slice(n_k, 2 * n_k)
        lgs = []
        m = jnp.full((2 * N_HEADS, 1), NEG_INF, F32)
        for c in range(n_chunks):
            cs = slice(c * SAMP_CHUNK, (c + 1) * SAMP_CHUNK)
            lg = _dot(q16, kv_bf[k_rows, cs]) + bias_chunk(c)
            lgs.append(lg)
            m = jnp.maximum(m, jnp.max(lg, axis=1, keepdims=True))
        new_bf = new_row.astype(MXU_DTYPE).astype(F32)
        lg_new = jnp.sum(q16.astype(F32) * new_bf[:, k_rows], axis=1, keepdims=True) + bias_new
        m = jnp.maximum(m, lg_new)
        l = jnp.zeros((2 * N_HEADS, 1), F32)
        acc = jnp.zeros((2 * N_HEADS, n_k), F32)
        for c in range(n_chunks):
            cs = slice(c * SAMP_CHUNK, (c + 1) * SAMP_CHUNK)
            p = jnp.exp(lgs[c] - m)
            l = l + jnp.sum(p, axis=1, keepdims=True)
            acc = acc + _dot_nt(p.astype(MXU_DTYPE), kv_bf[v_rows, cs])
        p_new = jnp.exp(lg_new - m)
        l = l + p_new
        acc = acc + p_new.astype(MXU_DTYPE).astype(F32) * new_bf[:, v_rows]
        return acc / l

    for c in range(n_chunks):
        cs = slice(c * SAMP_CHUNK, (c + 1) * SAMP_CHUNK)
        abf[:, cs] = abuf[slot, :, cs].astype(abf.dtype)
    oa_ref[0] = attend(qa_ref[0], abf, HEAD_DIM, anew_ref[0],
                       lambda c: bias_ref[0, :, c * SAMP_CHUNK:(c + 1) * SAMP_CHUNK],
                       bias_ref[0, :, past:past + 1])

    lf = lbuf[slot]
    sl = sl_ref[...]
    hi, mid, lo = _split3(lf)
    rev = _dot(hi, sl) + _dot(mid, sl) + _dot(lo, sl)
    tot = rev[:, 0:1] + lf[:, 0:1]
    run = jnp.zeros((N_HEADS, 1), F32)
    rp = [None] * n_pages
    for p in reversed(range(n_pages)):
        ps = slice(p * N_HEADS, (p + 1) * N_HEADS)
        rp[p] = rev[ps] + run
        run = run + tot[ps]
    nq = lfnew_ref[0]
    zpad = jnp.zeros((N_HEADS, SAMP_CHUNK), F32)

    def bias_b(c):
        rpc = jnp.concatenate(rp[c * pages_per_chunk:(c + 1) * pages_per_chunk], axis=1)
        return jnp.concatenate([rpc, zpad], axis=0) + nq

    for c in range(n_chunks):
        cs = slice(c * SAMP_CHUNK, (c + 1) * SAMP_CHUNK)
        bbf[:, cs] = bbuf[slot, :, cs].astype(bbf.dtype)
    out_b = attend(qb_ref[0], bbf, 2 * HEAD_DIM, bnew_ref[0], bias_b, jnp.zeros((2 * N_HEADS, 1), F32))
    head = lax.broadcasted_iota(jnp.int32, out_b.shape, 0)
    second_group = head >= N_HEADS // B_KV_HEADS
    ob_ref[0] = jnp.where(second_group, pltpu.roll(out_b, HEAD_DIM, 1), out_b)


def _samp_attn(page_table, qa16, qb16, bias, a_new, b_new, lf_new16, cache_a, cache_b, cache_lft):
    db, n_pages = page_table.shape
    past = n_pages * PAGE
    sl = jnp.tril(jnp.ones((LANES, LANES), F32), k=-1).astype(MXU_DTYPE)
    body = functools.partial(_samp_attn_body, n_pages=n_pages)
    row = lambda b, pt: (b, 0, 0)
    any_spec = pl.BlockSpec(memory_space=pl.ANY)
    return pl.pallas_call(
        body,
        grid_spec=pltpu.PrefetchScalarGridSpec(
            num_scalar_prefetch=1, grid=(db,),
            in_specs=[pl.BlockSpec((1,) + qa16.shape[1:], row), pl.BlockSpec((1,) + qb16.shape[1:], row),
                      pl.BlockSpec((1,) + bias.shape[1:], row), pl.BlockSpec((1,) + a_new.shape[1:], row),
                      pl.BlockSpec((1,) + b_new.shape[1:], row), pl.BlockSpec((1,) + lf_new16.shape[1:], row),
                      pl.BlockSpec(sl.shape, lambda b, pt: (0, 0)), any_spec, any_spec, any_spec],
            out_specs=[pl.BlockSpec((1, 2 * N_HEADS, HEAD_DIM), row),
                       pl.BlockSpec((1, 2 * N_HEADS, 2 * HEAD_DIM), row)],
            scratch_shapes=[pltpu.VMEM((2, 2 * HEAD_DIM, past), F32), pltpu.VMEM((2, 4 * HEAD_DIM, past), F32),
                            pltpu.VMEM((2, n_pages * N_HEADS, PAGE), F32),
                            pltpu.SemaphoreType.DMA((2,)), pltpu.SemaphoreType.DMA((2,)),
                            pltpu.SemaphoreType.DMA((2,)),
                            pltpu.VMEM((2 * HEAD_DIM, past), MXU_DTYPE), pltpu.VMEM((4 * HEAD_DIM, past), MXU_DTYPE)]),
        out_shape=[jax.ShapeDtypeStruct((db, 2 * N_HEADS, HEAD_DIM), F32),
                   jax.ShapeDtypeStruct((db, 2 * N_HEADS, 2 * HEAD_DIM), F32)],
        compiler_params=pltpu.CompilerParams(dimension_semantics=("arbitrary",), vmem_limit_bytes=VMEM_LIMIT),
        name="samp_attn",
    )(page_table.reshape(-1), qa16, qb16, bias, a_new, b_new, lf_new16, sl, cache_a, cache_b, cache_lft)


def _ffn_body(x_ref, o_ref, ge_ref, be_ref, wo_ref, g1_ref, b1_ref, wg_ref, wu_ref, wd_ref, g2_ref, b2_ref,
              y_ref, *, alpha, ff_chunk):
    h = _layer_norm(x_ref[...], ge_ref[...], be_ref[...])
    mixed = _dot(o_ref[...], wo_ref[...])
    h1 = _layer_norm(alpha * h + mixed, g1_ref[...], b1_ref[...])
    h1b = h1.astype(MXU_DTYPE)
    d_ff = wg_ref.shape[1]
    ffn = jnp.zeros(h1.shape, F32)
    for c in range(d_ff // ff_chunk):
        cs = slice(c * ff_chunk, (c + 1) * ff_chunk)
        gate = _dot(h1b, wg_ref[:, cs])
        up = _dot(h1b, wu_ref[:, cs])
        act = (gate * jax.nn.sigmoid(gate)) * up
        ffn = ffn + _dot(act.astype(MXU_DTYPE), wd_ref[cs, :])
    y_ref[...] = _layer_norm(alpha * h1 + ffn, g2_ref[...], b2_ref[...])


def _ffn_chunk(d_ff):
    for n in (11, 8, 6, 4, 2, 1):
        if d_ff % (n * LANES) == 0:
            return n * LANES
    raise ValueError(f"d_ff={d_ff} is not a multiple of {LANES}")


def _ffn_call(x2, o2, ln_e, w_o, ln1, w_gate, w_up, w_down, ln2, alpha, tm):
    m, d = x2.shape
    assert m % tm == 0
    d_ff = w_gate.shape[1]
    row = lambda i: (i, 0)
    const = lambda i: (0, 0)
    vec = pl.BlockSpec((1, d), const)
    resident = lambda shape: pl.BlockSpec(shape, const, pipeline_mode=pl.Buffered(1))
    body = functools.partial(_ffn_body, alpha=alpha, ff_chunk=_ffn_chunk(d_ff))
    return pl.pallas_call(
        body,
        grid=(m // tm,),
        in_specs=[pl.BlockSpec((tm, d), row), pl.BlockSpec((tm, o2.shape[1]), row), vec, vec,
                  resident(w_o.shape), vec, vec, resident(w_gate.shape), resident(w_up.shape),
                  resident(w_down.shape), vec, vec],
        out_specs=pl.BlockSpec((tm, d), row),
        out_shape=jax.ShapeDtypeStruct((m, d), F32),
        compiler_params=pltpu.CompilerParams(dimension_semantics=("arbitrary",), vmem_limit_bytes=VMEM_LIMIT),
        name="ffn",
    )(x2, o2, ln_e[0].reshape(1, d), ln_e[1].reshape(1, d), w_o, ln1[0].reshape(1, d), ln1[1].reshape(1, d),
      w_gate, w_up, w_down, ln2[0].reshape(1, d), ln2[1].reshape(1, d))


def _row_tile(n, cap):
    t = min(n, cap)
    while n % t:
        t //= 2
    return t


def kernel(x_prompt, x_sample, cache_a_kv, cache_idx_k, cache_b_kv, cache_b_logf, page_table, meta,
           ln_emb_g, ln_emb_b, w_in, b_forget, w_o, ln1_g, ln1_b, w_gate, w_up, w_down, ln2_g, ln2_b):
    depth = w_in.shape[0]
    assert depth == 1, "single-layer trunk only"
    bsz, seq, d = x_prompt.shape
    db, ds, _ = x_sample.shape
    assert ds == 1, "one new token per decode sequence"
    n_meta = meta.shape[0]
    n_pages = page_table.shape[1]
    past = n_pages * PAGE
    alpha = (2.0 * depth) ** 0.25
    topk_p = min(TOPK_MAX, seq // 4)
    topk_s = min(TOPK_MAX, (past + ds) // 4)

    sizes = (512, 64, 64, 512, 64, 8, 512, 128, 128, 8)
    offs = [0]
    for sz in sizes:
        offs.append(offs[-1] + sz)
    w0 = w_in[0]
    part = lambda i: w0[:, offs[i]:offs[i + 1]]
    qa_w, ka_w, va_w, qi_w, ki_w, wi_w, qb_w, kb_w, vb_w, fl_w = [part(i) for i in range(10)]
    pad_w = jnp.zeros((d, _C_KVB - _C_KIW - HEAD_DIM - 2 * N_HEADS), w0.dtype)
    w_cat = jnp.concatenate([qa_w, qi_w, qb_w, ka_w, va_w, ki_w, wi_w, fl_w, pad_w, kb_w, vb_w],
                            axis=1).astype(MXU_DTYPE)
    zero_c = jnp.zeros((1, N_HEADS), F32)
    proj = functools.partial(_proj_call, w_cat=w_cat, ln_g=ln_emb_g, ln_b=ln_emb_b, b_forget=b_forget[0])

    (_, wc_m, kidx_m, kva_m, kvb_m, akv_m, idxk_m, bkv_m, lf_m, kext_m) = proj(
        meta[None].astype(x_prompt.dtype), jnp.arange(n_meta), zero_c, tm=n_meta)
    c_meta = wc_m[0, :, N_HEADS:]
    (q_t, wc, kidx, kva, kvb, akv, idxk, bkv, lf, kext) = proj(
        x_prompt, n_meta + jnp.arange(seq), c_meta[n_meta - 1:n_meta], tm=_row_tile(seq, 512), q_token_minor=True)
    (qcat_s, wc_s, _, _, _, akv_s, idxk_s, bkv_s, lf_s, _) = proj(
        x_sample.reshape(1, db, d), jnp.full((db,), past, jnp.int32), zero_c, tm=_row_tile(db, 512))

    pad_rows = lambda t: jnp.pad(t[0], ((0, LANES - n_meta), (0, 0)))
    t_last = lambda t: jnp.swapaxes(t, -1, -2)
    vat, vbt = t_last(kva[:, :, HEAD_DIM:]), t_last(kvb[:, :, 2 * HEAD_DIM:])
    vat_m, vbt_m = t_last(pad_rows(kva_m)[:, HEAD_DIM:]), t_last(pad_rows(kvb_m)[:, 2 * HEAD_DIM:])
    o_t = _prompt_attn(q_t, t_last(wc), kidx, kva, kvb, vat, vbt, kext,
                       pad_rows(kidx_m), pad_rows(kva_m), pad_rows(kvb_m), vat_m, vbt_m, pad_rows(kext_m),
                       n_meta, topk_p)
    o_prompt = t_last(o_t)

    qs = qcat_s[0].astype(F32).reshape(db, 3, N_HEADS, HEAD_DIM)
    pad_heads = lambda t: jnp.pad(t, ((0, 0), (0, N_HEADS), (0, 0)))
    qa16 = pad_heads(qs[:, 0]).astype(MXU_DTYPE)
    qi16 = pad_heads(qs[:, 1]).astype(MXU_DTYPE)
    half = N_HEADS // B_KV_HEADS
    head_group = jnp.arange(N_HEADS)[:, None] // half
    lane_group = jnp.arange(2 * HEAD_DIM)[None, :] // HEAD_DIM
    qb16 = pad_heads(jnp.where(head_group == lane_group, jnp.concatenate([qs[:, 2], qs[:, 2]], axis=2),
                               0.0)).astype(MXU_DTYPE)
    wi16 = pad_heads(wc_s[0, :, :N_HEADS, None])
    lfn16 = pad_heads(lf_s[0][:, :, None])
    feat_major = lambda c: jnp.moveaxis(c[0], 1, -1).reshape(c.shape[1], -1, PAGE)
    scores = _samp_scores(page_table, qi16, wi16, idxk_s[0][:, None, :], feat_major(cache_idx_k))
    n_t = n_pages + 1
    scores_tt = jnp.transpose(scores.reshape(db, n_t, PAGE), (1, 2, 0))
    selb_tt = _samp_select(scores_tt, n_pages, topk_s)
    bias = jnp.transpose(selb_tt, (2, 0, 1)).reshape(db, 1, n_t * PAGE)
    oa_s, ob_s = _samp_attn(page_table, qa16, qb16, bias, akv_s[0][:, None, :], bkv_s[0][:, None, :], lfn16,
                            feat_major(cache_a_kv), feat_major(cache_b_kv), feat_major(cache_b_logf))
    heads_flat = lambda t: t[:, :N_HEADS, :HEAD_DIM].reshape(db, N_HEADS * HEAD_DIM)
    o_sample = jnp.concatenate([heads_flat(oa_s), heads_flat(ob_s)], axis=1).astype(MXU_DTYPE)

    cast = lambda w: w[0].astype(MXU_DTYPE)
    ffn = functools.partial(_ffn_call, ln_e=(ln_emb_g, ln_emb_b), w_o=cast(w_o), ln1=(ln1_g[0], ln1_b[0]),
                            w_gate=cast(w_gate), w_up=cast(w_up), w_down=cast(w_down), ln2=(ln2_g[0], ln2_b[0]),
                            alpha=alpha)
    y_prompt = ffn(x_prompt.reshape(bsz * seq, d), o_prompt.reshape(bsz * seq, -1),
                   tm=_row_tile(bsz * seq, 512)).reshape(bsz, seq, d)
    y_sample = ffn(x_sample.reshape(db, d), o_sample, tm=_row_tile(db, 512)).reshape(db, ds, d)

    def with_meta(m_part, x_part, tail):
        m_b = jnp.broadcast_to(m_part, (bsz,) + m_part.shape[1:])
        return jnp.concatenate([m_b, x_part], axis=1).reshape((1, bsz, n_meta + seq) + tail)

    return (y_prompt, y_sample,
            with_meta(akv_m, akv, (2, 1, HEAD_DIM)),
            with_meta(idxk_m, idxk, (HEAD_DIM,)),
            with_meta(bkv_m, bkv, (2, B_KV_HEADS, HEAD_DIM)),
            with_meta(lf_m, lf, (N_HEADS,)),
            akv_s[0].reshape(1, db, ds, 2, 1, HEAD_DIM),
            idxk_s[0].reshape(1, db, ds, HEAD_DIM),
            bkv_s[0].reshape(1, db, ds, 2, B_KV_HEADS, HEAD_DIM),
            lf_s[0].reshape(1, db, ds, N_HEADS))
```
